```python
import jax, jax.numpy as jnp
from jax import lax
import numpy as np

D_MODEL = 1024
BATCH = 16
SEQ = 2048
DEPTH = 2

D_FF = 2816
NORM_EPS = 1e-6
FFN_RES_WEIGHT = 0.5

D_CONV = D_MODEL
CONV_WIDTH = 31
CONV_LN_EPS = 1e-5

MLA_HEADS = 8
Q_LORA = 512
KV_LORA = 256
QK_NOPE = 128
QK_ROPE = 64
V_HEAD = 128
ROPE_THETA = 10000.0
Q_BLOCK = 128

IN_COLS = 2 * D_CONV + Q_LORA + KV_LORA + QK_ROPE
OUT_COLS = D_CONV + MLA_HEADS * V_HEAD

RWKV_HEAD = 64
RWKV_HEADS = D_MODEL // RWKV_HEAD
DECAY_LORA = 64
AAA_LORA = 64
GATE_LORA = 128
RWKV_GN_EPS = 64e-5
N_SHIFT_MIX = 6

kernel_name = 'hybrid_conv_mla_rwkv7_macaron'


def rms_norm(x, g, eps=NORM_EPS):
    xf = x.astype(jnp.float32)
    y = xf * lax.rsqrt(jnp.mean(xf * xf, axis=-1, keepdims=True) + eps)
    return (y * g.astype(jnp.float32)).astype(x.dtype)


def swiglu_ffn(h, w_gate, w_up, w_down):
    return (jax.nn.silu(h @ w_gate) * (h @ w_up)) @ w_down


def rope_tables(positions):
    inv_freq = 1.0 / (ROPE_THETA ** (jnp.arange(0, QK_ROPE, 2, dtype=jnp.float32) / QK_ROPE))
    ang = positions.astype(jnp.float32)[..., None] * inv_freq
    return jnp.cos(ang), jnp.sin(ang)


def apply_rope(x, cos, sin):
    xf = x.astype(jnp.float32)
    x1, x2 = jnp.split(xf, 2, axis=-1)
    return jnp.concatenate([x1 * cos - x2 * sin, x1 * sin + x2 * cos], axis=-1).astype(x.dtype)


def conformer_conv(a, gate, conv_w, conv_b, ln_g, ln_b):
    h = a * jax.nn.sigmoid(gate)
    h = lax.conv_general_dilated(
        h, conv_w[:, None, :].astype(h.dtype), window_strides=(1,),
        padding=[(CONV_WIDTH - 1, 0)],
        dimension_numbers=('NWC', 'WIO', 'NWC'),
        feature_group_count=D_CONV) + conv_b
    hf = h.astype(jnp.float32)
    mu = jnp.mean(hf, axis=-1, keepdims=True)
    var = jnp.mean(jnp.square(hf - mu), axis=-1, keepdims=True)
    hn = (hf - mu) * lax.rsqrt(var + CONV_LN_EPS) * ln_g + ln_b
    return jax.nn.silu(hn).astype(a.dtype)


def mla_attention(q_nope, q_pe, k_nope, k_pe, v):
    seq = q_nope.shape[1]
    scale = (QK_NOPE + QK_ROPE) ** -0.5
    outs = []
    for blk in range(seq // Q_BLOCK):
        q0, q1 = blk * Q_BLOCK, (blk + 1) * Q_BLOCK
        s = (jnp.einsum('bqhd,bkhd->bhqk', q_nope[:, q0:q1], k_nope[:, :q1])
             + jnp.einsum('bqhr,bkr->bhqk', q_pe[:, q0:q1], k_pe[:, :q1]))
        s = s.astype(jnp.float32) * scale
        causal = (q0 + jnp.arange(Q_BLOCK))[:, None] >= jnp.arange(q1)[None, :]
        p = jax.nn.softmax(jnp.where(causal, s, -jnp.inf), axis=-1).astype(v.dtype)
        outs.append(jnp.einsum('bhqk,bkhd->bqhd', p, v[:, :q1]))
    return jnp.concatenate(outs, axis=1)


def conv_mla_mixer(h, cos, sin, w_in, conv_w, conv_b, conv_ln_g, conv_ln_b,
                   q_norm, w_uq, kv_norm, w_ukv, w_out):
    b, s, _ = h.shape
    z = h @ w_in
    conv_a, conv_gate, q_lat, kv_lat, k_pe = jnp.split(
        z, [D_CONV, 2 * D_CONV, 2 * D_CONV + Q_LORA, 2 * D_CONV + Q_LORA + KV_LORA], axis=-1)
    conv_out = conformer_conv(conv_a, conv_gate, conv_w, conv_b, conv_ln_g, conv_ln_b)
    q = (rms_norm(q_lat, q_norm) @ w_uq).reshape(b, s, MLA_HEADS, QK_NOPE + QK_ROPE)
    q_nope = q[..., :QK_NOPE]
    q_pe = apply_rope(q[..., QK_NOPE:], cos[:, :, None, :], sin[:, :, None, :])
    kv = (rms_norm(kv_lat, kv_norm) @ w_ukv).reshape(b, s, MLA_HEADS, QK_NOPE + V_HEAD)
    k_nope, v = kv[..., :QK_NOPE], kv[..., QK_NOPE:]
    k_pe = apply_rope(k_pe, cos, sin)
    attn = mla_attention(q_nope, q_pe, k_nope, k_pe, v).reshape(b, s, MLA_HEADS * V_HEAD)
    return jnp.concatenate([conv_out, attn], axis=-1) @ w_out


def rwkv7_time_mix(h, time_mu, w_r, w_k, w_v, w_o, w0, w1, w2, a0, a1, a2,
                   g1, g2, k_k, k_a, r_k, ln_x_g, ln_x_b):
    b, s, d = h.shape
    f32 = jnp.float32
    hh = jnp.pad(h, ((0, 0), (1, 0), (0, 0)))[:, :-1] - h
    xr, xw, xk, xv, xa, xg = [h + hh * time_mu[i] for i in range(N_SHIFT_MIX)]
    r = (xr @ w_r).astype(f32)
    k = (xk @ w_k).astype(f32)
    v = (xv @ w_v).astype(f32)
    w_log = -jax.nn.softplus(-(w0 + jnp.tanh(xw @ w1) @ w2).astype(f32)) - 0.5
    decay = jnp.exp(-jnp.exp(w_log))
    a = jax.nn.sigmoid((a0 + (xa @ a1) @ a2).astype(f32))
    g = (jax.nn.sigmoid(xg @ g1) @ g2).astype(f32)
    kk = (k * k_k).reshape(b, s, RWKV_HEADS, RWKV_HEAD)
    kk = kk / jnp.maximum(jnp.linalg.norm(kk, axis=-1, keepdims=True), 1e-12)
    k = k * (1.0 + (a - 1.0) * k_a)
    heads = lambda t: t.reshape(b, s, RWKV_HEADS, RWKV_HEAD)
    r, k, v, decay, a = heads(r), heads(k), heads(v), heads(decay), heads(a)

    def step(state, inp):
        r_t, w_t, k_t, v_t, kk_t, a_t = inp
        sa = jnp.einsum('bhvk,bhk->bhv', state, -kk_t)
        state = (state * w_t[:, :, None, :]
                 + sa[..., None] * (kk_t * a_t)[:, :, None, :]
                 + v_t[..., None] * k_t[:, :, None, :])
        return state, jnp.einsum('bhvk,bhk->bhv', state, r_t)

    xs = tuple(jnp.moveaxis(t, 1, 0) for t in (r, decay, k, v, kk, a))
    state0 = jnp.zeros((b, RWKV_HEADS, RWKV_HEAD, RWKV_HEAD), f32)
    _, y = lax.scan(step, state0, xs)
    y = jnp.moveaxis(y, 0, 1)
    mu = jnp.mean(y, axis=-1, keepdims=True)
    var = jnp.mean(jnp.square(y - mu), axis=-1, keepdims=True)
    y = ((y - mu) * lax.rsqrt(var + RWKV_GN_EPS)).reshape(b, s, d) * ln_x_g + ln_x_b
    bonus = jnp.sum(r * k * r_k, axis=-1, keepdims=True) * v
    y = y + bonus.reshape(b, s, d)
    return (y * g).astype(h.dtype) @ w_o


def setup_inputs(seed: int = 0) -> dict:
    key = jax.random.key(seed)
    keys = iter(jax.random.split(key, 48))
    f32 = jnp.float32
    nrm = lambda shape, scale: jax.random.normal(next(keys), shape, f32) * scale
    gain = lambda shape: 1.0 + nrm(shape, 0.02)
    ne, no = (DEPTH + 1) // 2, DEPTH // 2
    x = jax.random.normal(next(keys), (BATCH, SEQ, D_MODEL), f32)
    offset = jax.random.randint(next(keys), (BATCH, 1), 0, 1024, dtype=jnp.int32)
    positions = jnp.arange(SEQ, dtype=jnp.int32)[None, :] + offset
    return {
        'x': x,
        'positions': positions,
        'ffn_norm': gain((DEPTH, 2, D_MODEL)),
        'ffn_w_gate': nrm((DEPTH, 2, D_MODEL, D_FF), D_MODEL ** -0.5),
        'ffn_w_up': nrm((DEPTH, 2, D_MODEL, D_FF), D_MODEL ** -0.5),
        'ffn_w_down': nrm((DEPTH, 2, D_FF, D_MODEL), D_FF ** -0.5),
        'mix_norm_even': gain((ne, D_MODEL)),
        'w_in': nrm((ne, D_MODEL, IN_COLS), D_MODEL ** -0.5),
        'conv_w': nrm((ne, CONV_WIDTH, D_CONV), CONV_WIDTH ** -0.5),
        'conv_b': nrm((ne, D_CONV), 0.01),
        'conv_ln_g': gain((ne, D_CONV)),
        'conv_ln_b': nrm((ne, D_CONV), 0.01),
        'q_norm': gain((ne, Q_LORA)),
        'w_uq': nrm((ne, Q_LORA, MLA_HEADS * (QK_NOPE + QK_ROPE)), Q_LORA ** -0.5),
        'kv_norm': gain((ne, KV_LORA)),
        'w_ukv': nrm((ne, KV_LORA, MLA_HEADS * (QK_NOPE + V_HEAD)), KV_LORA ** -0.5),
        'w_out': nrm((ne, OUT_COLS, D_MODEL), OUT_COLS ** -0.5),
        'mix_norm_odd': gain((no, D_MODEL)),
        'time_mu': jax.random.uniform(next(keys), (no, N_SHIFT_MIX, D_MODEL), f32),
        'w_r': nrm((no, D_MODEL, D_MODEL), D_MODEL ** -0.5),
        'w_k': nrm((no, D_MODEL, D_MODEL), D_MODEL ** -0.5),
        'w_v': nrm((no, D_MODEL, D_MODEL), D_MODEL ** -0.5),
        'w_o': nrm((no, D_MODEL, D_MODEL), D_MODEL ** -0.5),
        'w0': jax.random.uniform(next(keys), (no, D_MODEL), f32, -5.0, -0.5),
        'w1': nrm((no, D_MODEL, DECAY_LORA), D_MODEL ** -0.5),
        'w2': nrm((no, DECAY_LORA, D_MODEL), DECAY_LORA ** -0.5),
        'a0': nrm((no, D_MODEL), 0.1),
        'a1': nrm((no, D_MODEL, AAA_LORA), D_MODEL ** -0.5),
        'a2': nrm((no, AAA_LORA, D_MODEL), AAA_LORA ** -0.5),
        'g1': nrm((no, D_MODEL, GATE_LORA), D_MODEL ** -0.5),
        'g2': nrm((no, GATE_LORA, D_MODEL), GATE_LORA ** -0.5),
        'k_k': 0.85 + nrm((no, D_MODEL), 0.02),
        'k_a': gain((no, D_MODEL)),
        'r_k': nrm((no, RWKV_HEADS, RWKV_HEAD), 0.1),
        'ln_x_g': gain((no, D_MODEL)),
        'ln_x_b': nrm((no, D_MODEL), 0.01),
        'final_norm': gain((D_MODEL,)),
    }


def reference(x, positions, ffn_norm, ffn_w_gate, ffn_w_up, ffn_w_down,
              mix_norm_even, w_in, conv_w, conv_b, conv_ln_g, conv_ln_b,
              q_norm, w_uq, kv_norm, w_ukv, w_out,
              mix_norm_odd, time_mu, w_r, w_k, w_v, w_o, w0, w1, w2,
              a0, a1, a2, g1, g2, k_k, k_a, r_k, ln_x_g, ln_x_b, final_norm):
    cos, sin = rope_tables(positions)
    for layer in range(DEPTH):
        x = x + FFN_RES_WEIGHT * swiglu_ffn(rms_norm(x, ffn_norm[layer, 0]),
                                            ffn_w_gate[layer, 0], ffn_w_up[layer, 0],
                                            ffn_w_down[layer, 0])
        if layer % 2 == 0:
            e = layer // 2
            x = x + conv_mla_mixer(rms_norm(x, mix_norm_even[e]), cos, sin, w_in[e],
                                   conv_w[e], conv_b[e], conv_ln_g[e], conv_ln_b[e],
                                   q_norm[e], w_uq[e], kv_norm[e], w_ukv[e], w_out[e])
        else:
            o = layer // 2
            x = x + rwkv7_time_mix(rms_norm(x, mix_norm_odd[o]), time_mu[o], w_r[o], w_k[o],
                                   w_v[o], w_o[o], w0[o], w1[o], w2[o], a0[o], a1[o],
                                   a2[o], g1[o], g2[o], k_k[o], k_a[o], r_k[o],
                                   ln_x_g[o], ln_x_b[o])
        x = x + FFN_RES_WEIGHT * swiglu_ffn(rms_norm(x, ffn_norm[layer, 1]),
                                            ffn_w_gate[layer, 1], ffn_w_up[layer, 1],
                                            ffn_w_down[layer, 1])
    return rms_norm(x, final_norm)
```

```python
import functools
import math

import jax
import jax.numpy as jnp
from jax import lax
from jax.experimental import pallas as pl
from jax.experimental.pallas import tpu as pltpu

F32 = jnp.float32
BF16 = jnp.bfloat16

NORM_EPS = 1e-6
FFN_RES_WEIGHT = 0.5
CONV_LN_EPS = 1e-5
MLA_HEADS = 8
QK_NOPE = 128
QK_ROPE = 64
V_HEAD = 128
QK_DIM = QK_NOPE + QK_ROPE
ROPE_THETA = 10000.0
RWKV_HEAD = 64
RWKV_GN_EPS = 64e-5

LANES = 128
VMEM_LIMIT = 56 * 1024 * 1024
NEG_BIG = -1e30

SCAN_CHUNK = 64
SCAN_HEADS = 2


def _params(sem):
    return pltpu.CompilerParams(dimension_semantics=sem, vmem_limit_bytes=VMEM_LIMIT)


def _dot(a, b):
    return jnp.dot(a, b, preferred_element_type=F32)


def _dot_nt(a, b):
    return lax.dot_general(a, b, (((1,), (1,)), ((), ())), preferred_element_type=F32)


def _dot_tn(a, b):
    return lax.dot_general(a, b, (((0,), (0,)), ((), ())), preferred_element_type=F32)


def _rms(x, g, eps):
    ms = jnp.mean(x * x, axis=-1, keepdims=True)
    return x * lax.rsqrt(ms + eps) * g


def _hilo(x):
    hi = x.astype(BF16)
    lo = (x - hi.astype(F32)).astype(BF16)
    return hi, lo


def _const_spec(shape):
    nd = len(shape)
    return pl.BlockSpec(shape, lambda *_: (0,) * nd)


def _ffn_body(*refs, final):
    if final:
        x_ref, g_ref, wg_ref, wu_ref, wd_ref, fg_ref, o_ref, h_scr, acc_scr = refs
    else:
        x_ref, g_ref, wg_ref, wu_ref, wd_ref, o_ref, h_scr, acc_scr = refs
    j = pl.program_id(1)

    @pl.when(j == 0)
    def _():
        h_scr[...] = _rms(x_ref[...], g_ref[...], NORM_EPS).astype(BF16)
        acc_scr[...] = jnp.zeros_like(acc_scr)

    h = h_scr[...]
    gate = _dot(h, wg_ref[...])
    up = _dot(h, wu_ref[...])
    act = (gate * jax.nn.sigmoid(gate) * up).astype(BF16)
    acc_scr[...] += _dot(act, wd_ref[...])

    @pl.when(j == pl.num_programs(1) - 1)
    def _():
        y = x_ref[...] + FFN_RES_WEIGHT * acc_scr[...]
        if final:
            y = _rms(y, fg_ref[...], NORM_EPS)
        o_ref[...] = y


def _ffn(x2, g, wg, wu, wd, final_g=None, *, tm=1024, tf=256):
    t, d = x2.shape
    ff = wg.shape[1]
    tm = min(tm, t)
    final = final_g is not None
    in_specs = [
        pl.BlockSpec((tm, d), lambda i, j: (i, 0)),
        pl.BlockSpec((1, d), lambda i, j: (0, 0)),
        pl.BlockSpec((d, tf), lambda i, j: (0, j)),
        pl.BlockSpec((d, tf), lambda i, j: (0, j)),
        pl.BlockSpec((tf, d), lambda i, j: (j, 0)),
    ]
    args = [x2, g.reshape(1, d), wg, wu, wd]
    if final:
        in_specs.append(pl.BlockSpec((1, d), lambda i, j: (0, 0)))
        args.append(final_g.reshape(1, d))
    return pl.pallas_call(
        functools.partial(_ffn_body, final=final),
        grid=(t // tm, ff // tf),
        in_specs=in_specs,
        out_specs=pl.BlockSpec((tm, d), lambda i, j: (i, 0)),
        out_shape=jax.ShapeDtypeStruct((t, d), F32),
        scratch_shapes=[pltpu.VMEM((tm, d), BF16), pltpu.VMEM((tm, d), F32)],
        compiler_params=_params(("parallel", "arbitrary")),
        name="ffn_final" if final else "ffn",
    )(*args)


CONV_HALO = 32
CONV_ROWS = 64


def _mix_in_body(x_ref, pos_ref, g_ref, wa_ref, wgt_ref, wql_ref, wkvl_ref, wkpe_ref, wkpes_ref,
                 cw_ref, cb_ref, lng_ref, lnb_ref, qn_ref, wqn_ref, wqp_ref, wqps_ref,
                 kvn_ref, wukv_ref, invf_ref, sgn_ref,
                 conv_ref, q_ref, k_ref, v_ref, ext_scr, acc_scr, *, tm, width):
    si = pl.program_id(1)
    d_conv = ext_scr.shape[1]
    hn = _rms(x_ref[0], g_ref[...], NORM_EPS).astype(BF16)

    za = _dot(hn, wa_ref[...])
    zg = _dot(hn, wgt_ref[...])

    @pl.when(si == 0)
    def _():
        ext_scr[0:CONV_HALO, :] = jnp.zeros((CONV_HALO, d_conv), F32)

    @pl.when(si > 0)
    def _():
        ext_scr[0:CONV_HALO, :] = ext_scr[tm:tm + CONV_HALO, :]

    ext_scr[CONV_HALO:CONV_HALO + tm, :] = za * jax.nn.sigmoid(zg)

    base = CONV_HALO - (width - 1)

    def col_block(c, carry):
        cols = pl.ds(pl.multiple_of(c * LANES, LANES), LANES)
        w = cw_ref[:, cols]
        b = cb_ref[:, cols]
        for r0 in range(0, tm, CONV_ROWS):
            acc = jnp.broadcast_to(b, (CONV_ROWS, LANES))
            for j in range(width):
                acc = acc + w[j:j + 1, :] * ext_scr[r0 + base + j:r0 + base + j + CONV_ROWS, cols]
            acc_scr[r0:r0 + CONV_ROWS, cols] = acc
        return carry

    lax.fori_loop(0, d_conv // LANES, col_block, 0)
    hc = acc_scr[...]
    mu = jnp.mean(hc, axis=-1, keepdims=True)
    dc = hc - mu
    var = jnp.mean(dc * dc, axis=-1, keepdims=True)
    hcn = dc * lax.rsqrt(var + CONV_LN_EPS) * lng_ref[...] + lnb_ref[...]
    conv_ref[0] = (hcn * jax.nn.sigmoid(hcn)).astype(BF16)

    ang = pos_ref[0].astype(F32) * invf_ref[...]
    cos1 = jnp.cos(ang)
    sin1 = jnp.sin(ang) * sgn_ref[...]
    n_pe = MLA_HEADS * QK_ROPE
    cosq = jnp.concatenate([cos1] * (n_pe // LANES), axis=-1)
    sinq = jnp.concatenate([sin1] * (n_pe // LANES), axis=-1)

    qn = _rms(_dot(hn, wql_ref[...]), qn_ref[...], NORM_EPS).astype(BF16)
    q_nope = _dot(qn, wqn_ref[...])
    q_pe = _dot(qn, wqp_ref[...]) * cosq + _dot(qn, wqps_ref[...]) * sinq

    kvn = _rms(_dot(hn, wkvl_ref[...]), kvn_ref[...], NORM_EPS).astype(BF16)
    kv = _dot(kvn, wukv_ref[...])
    k_pe = _dot(hn, wkpe_ref[...]) * cos1 + _dot(hn, wkpes_ref[...]) * sin1
    k_pe = k_pe[:, 0:QK_ROPE].astype(BF16)

    for h in range(MLA_HEADS):
        q_ref[0, h, :, 0:QK_NOPE] = q_nope[:, h * QK_NOPE:(h + 1) * QK_NOPE].astype(BF16)
        q_ref[0, h, :, QK_NOPE:QK_DIM] = q_pe[:, h * QK_ROPE:(h + 1) * QK_ROPE].astype(BF16)
        c0 = h * (QK_NOPE + V_HEAD)
        k_ref[0, h, :, 0:QK_NOPE] = kv[:, c0:c0 + QK_NOPE].astype(BF16)
        k_ref[0, h, :, QK_NOPE:QK_DIM] = k_pe
        v_ref[0, h] = kv[:, c0 + QK_NOPE:c0 + QK_NOPE + V_HEAD].astype(BF16)


def _swap_halves(w, block):
    k, n = w.shape
    w = w.reshape(k, n // block, 2, block // 2)
    return w[:, :, ::-1, :].reshape(k, n)


def _mix_in(x, positions, norm_g, w_in, conv_w, conv_b, ln_g, ln_b, q_norm, w_uq, kv_norm, w_ukv, *, tm=512):
    b, s, d = x.shape
    tm = min(tm, s)
    width, d_conv = conv_w.shape
    q_lora = q_norm.shape[0]
    kv_lora = kv_norm.shape[0]
    assert width - 1 <= CONV_HALO and tm % CONV_ROWS == 0 and tm >= CONV_HALO
    o1, o2, o3 = 2 * d_conv, 2 * d_conv + q_lora, 2 * d_conv + q_lora + kv_lora
    w_a, w_gt = w_in[:, :d_conv].astype(BF16), w_in[:, d_conv:o1].astype(BF16)
    w_ql, w_kvl = w_in[:, o1:o2].astype(BF16), w_in[:, o2:o3].astype(BF16)
    w_kpe = w_in[:, o3:]
    pad = jnp.zeros((d, LANES - QK_ROPE), F32)
    w_kpes = jnp.concatenate([_swap_halves(w_kpe, QK_ROPE), pad], axis=1).astype(BF16)
    w_kpe = jnp.concatenate([w_kpe, pad], axis=1).astype(BF16)
    wq = w_uq.reshape(q_lora, MLA_HEADS, QK_DIM)
    w_qn = wq[:, :, :QK_NOPE].reshape(q_lora, MLA_HEADS * QK_NOPE).astype(BF16)
    w_qp = wq[:, :, QK_NOPE:].reshape(q_lora, MLA_HEADS * QK_ROPE)
    w_qps = _swap_halves(w_qp, QK_ROPE).astype(BF16)
    w_qp = w_qp.astype(BF16)
    inv_freq = 1.0 / (ROPE_THETA ** (jnp.arange(0, QK_ROPE, 2, dtype=F32) / QK_ROPE))
    invf = jnp.tile(inv_freq, 2 * LANES // QK_ROPE).reshape(1, LANES)
    half = QK_ROPE // 2
    sgn = jnp.tile(jnp.concatenate([-jnp.ones((half,), F32), jnp.ones((half,), F32)]), LANES // QK_ROPE).reshape(1, LANES)

    row = lambda v: v.reshape(1, -1)
    consts = [row(norm_g), w_a, w_gt, w_ql, w_kvl, w_kpe, w_kpes, conv_w, row(conv_b), row(ln_g), row(ln_b),
              row(q_norm), w_qn, w_qp, w_qps, row(kv_norm), w_ukv.astype(BF16), invf, sgn]
    in_specs = [pl.BlockSpec((1, tm, d), lambda bi, si: (bi, si, 0)),
                pl.BlockSpec((1, tm, 1), lambda bi, si: (bi, si, 0))]
    in_specs += [_const_spec(c.shape) for c in consts]
    hb = lambda bi, si: (bi, 0, si, 0)
    return pl.pallas_call(
        functools.partial(_mix_in_body, tm=tm, width=width),
        grid=(b, s // tm),
        in_specs=in_specs,
        out_specs=[pl.BlockSpec((1, tm, d_conv), lambda bi, si: (bi, si, 0)),
                   pl.BlockSpec((1, MLA_HEADS, tm, QK_DIM), hb),
                   pl.BlockSpec((1, MLA_HEADS, tm, QK_DIM), hb),
                   pl.BlockSpec((1, MLA_HEADS, tm, V_HEAD), hb)],
        out_shape=[jax.ShapeDtypeStruct((b, s, d_conv), BF16),
                   jax.ShapeDtypeStruct((b, MLA_HEADS, s, QK_DIM), BF16),
                   jax.ShapeDtypeStruct((b, MLA_HEADS, s, QK_DIM), BF16),
                   jax.ShapeDtypeStruct((b, MLA_HEADS, s, V_HEAD), BF16)],
        scratch_shapes=[pltpu.VMEM((tm + CONV_HALO, d_conv), F32), pltpu.VMEM((tm, d_conv), F32)],
        compiler_params=_params(("parallel", "arbitrary")),
        name="mix_in",
    )(x, positions.reshape(b, s, 1), *consts)


def _attn_body(q_ref, k_ref, v_ref, o_ref, *, tq, exp2_scale):
    qi = pl.program_id(2)
    q = q_ref[0, 0]

    def block(j, carry, diagonal):
        m, l, acc = carry
        rows = pl.ds(pl.multiple_of(j * tq, tq), tq)
        s = _dot_nt(q, k_ref[0, 0, rows, :])
        if diagonal:
            r = lax.broadcasted_iota(jnp.int32, (tq, tq), 0)
            c = lax.broadcasted_iota(jnp.int32, (tq, tq), 1)
            s = jnp.where(r >= c, s, NEG_BIG)
        m_new = jnp.maximum(m, jnp.max(s, axis=-1, keepdims=True))
        p = jnp.exp2((s - m_new) * exp2_scale)
        alpha = jnp.exp2((m - m_new) * exp2_scale)
        l = alpha * l + jnp.sum(p, axis=-1, keepdims=True)
        acc = alpha * acc + _dot(p.astype(BF16), v_ref[0, 0, rows, :])
        return m_new, l, acc

    init = (jnp.full((tq, 1), NEG_BIG, F32), jnp.zeros((tq, 1), F32), jnp.zeros((tq, V_HEAD), F32))
    carry = lax.fori_loop(0, qi, lambda j, c: block(j, c, False), init)
    _, l, acc = block(qi, carry, True)
    o_ref[0] = (acc / l).astype(BF16)


def _attention(q, k, v, *, tq=512):
    b, h, s, _ = q.shape
    tq = min(tq, s)
    exp2_scale = (QK_DIM ** -0.5) * math.log2(math.e)
    return pl.pallas_call(
        functools.partial(_attn_body, tq=tq, exp2_scale=exp2_scale),
        grid=(b, h, s // tq),
        in_specs=[pl.BlockSpec((1, 1, tq, QK_DIM), lambda bi, hi, qi: (bi, hi, qi, 0)),
                  pl.BlockSpec((1, 1, s, QK_DIM), lambda bi, hi, qi: (bi, hi, 0, 0)),
                  pl.BlockSpec((1, 1, s, V_HEAD), lambda bi, hi, qi: (bi, hi, 0, 0))],
        out_specs=pl.BlockSpec((1, tq, V_HEAD), lambda bi, hi, qi: (bi, qi, hi)),
        out_shape=jax.ShapeDtypeStruct((b, s, h * V_HEAD), BF16),
        compiler_params=_params(("parallel", "parallel", "arbitrary")),
        name="attention",
    )(q, k, v)


def _mix_out_body(x_ref, c_ref, a_ref, wc_ref, wa_ref, o_ref):
    o_ref[...] = x_ref[...] + _dot(c_ref[...], wc_ref[...]) + _dot(a_ref[...], wa_ref[...])


def _mix_out(x2, conv2, attn2, w_out, *, tm=1024):
    t, d = x2.shape
    tm = min(tm, t)
    dc, da = conv2.shape[1], attn2.shape[1]
    wc, wa = w_out[:dc].astype(BF16), w_out[dc:].astype(BF16)
    return pl.pallas_call(
        _mix_out_body,
        grid=(t // tm,),
        in_specs=[pl.BlockSpec((tm, d), lambda i: (i, 0)),
                  pl.BlockSpec((tm, dc), lambda i: (i, 0)),
                  pl.BlockSpec((tm, da), lambda i: (i, 0)),
                  _const_spec(wc.shape), _const_spec(wa.shape)],
        out_specs=pl.BlockSpec((tm, d), lambda i: (i, 0)),
        out_shape=jax.ShapeDtypeStruct((t, d), F32),
        compiler_params=_params(("parallel",)),
        name="mix_out",
    )(x2, conv2, attn2, wc, wa)


SHIFT_HALO = 8


def _headsum(x, e_ref, et_ref):
    hi, lo = _hilo(x)
    s = _dot(hi, e_ref[...]) + _dot(lo, e_ref[...])
    shi, slo = _hilo(s)
    return _dot(shi, et_ref[...]) + _dot(slo, et_ref[...])


def _rwkv_in_body(x_ref, g_ref, mu_ref, wr_ref, wk_ref, wv_ref, w0_ref, w1_ref, w2_ref, a0_ref, a1_ref, a2_ref,
                  g1_ref, g2_ref, kk_ref, ka_ref, rk_ref, e_ref, et_ref,
                  r_out, lw_out, k_out, v_out, kk_out, b_out, bonus_out, g_out, ext_scr, *, tm):
    si = pl.program_id(1)
    d = ext_scr.shape[1]
    h = _rms(x_ref[0], g_ref[...], NORM_EPS)

    @pl.when(si == 0)
    def _():
        ext_scr[0:SHIFT_HALO, :] = jnp.zeros((SHIFT_HALO, d), F32)

    @pl.when(si > 0)
    def _():
        ext_scr[0:SHIFT_HALO, :] = ext_scr[tm:tm + SHIFT_HALO, :]

    ext_scr[SHIFT_HALO:SHIFT_HALO + tm, :] = h
    hh = ext_scr[SHIFT_HALO - 1:SHIFT_HALO - 1 + tm, :] - h
    mix = lambda i: (h + hh * mu_ref[i:i + 1, :]).astype(BF16)
    xr, xw, xk, xv, xa, xg = [mix(i) for i in range(6)]
    r = _dot(xr, wr_ref[...])
    k = _dot(xk, wk_ref[...])
    v = _dot(xv, wv_ref[...])
    z = w0_ref[...] + _dot(jnp.tanh(_dot(xw, w1_ref[...])).astype(BF16), w2_ref[...])
    lw = -math.exp(-0.5) * jax.nn.sigmoid(z)
    a = jax.nn.sigmoid(a0_ref[...] + _dot(_dot(xa, a1_ref[...]).astype(BF16), a2_ref[...]))
    g = _dot(jax.nn.sigmoid(_dot(xg, g1_ref[...])).astype(BF16), g2_ref[...])
    kkr = k * kk_ref[...]
    kk = kkr / jnp.maximum(jnp.sqrt(_headsum(kkr * kkr, e_ref, et_ref)), 1e-12)
    k2 = k * (1.0 + (a - 1.0) * ka_ref[...])
    r_out[0] = r
    lw_out[0] = lw
    k_out[0] = k2
    v_out[0] = v
    kk_out[0] = kk
    b_out[0] = kk * a
    bonus_out[0] = _headsum(r * k2 * rk_ref[...], e_ref, et_ref) * v
    g_out[0] = g


def _pad_cols(w, n):
    return jnp.pad(w, ((0, 0), (0, n - w.shape[1])))


def _pad_rows(w, n):
    return jnp.pad(w, ((0, n - w.shape[0]), (0, 0)))


def _head_onehot(d):
    heads = d // RWKV_HEAD
    e = (jnp.arange(d)[:, None] // RWKV_HEAD == jnp.arange(LANES)[None, :]).astype(BF16)
    assert heads <= LANES
    return e, e.T


def _rwkv_in(x, norm_g, time_mu, w_r, w_k, w_v, w0, w1, w2, a0, a1, a2, g1, g2, k_k, k_a, r_k, *, tm=256):
    b, s, d = x.shape
    tm = min(tm, s)
    row = lambda v: v.reshape(1, -1)
    lp = lambda n: ((n + LANES - 1) // LANES) * LANES
    e, et = _head_onehot(d)
    consts = [row(norm_g), time_mu, w_r.astype(BF16), w_k.astype(BF16), w_v.astype(BF16),
              row(w0), _pad_cols(w1, lp(w1.shape[1])).astype(BF16), _pad_rows(w2, lp(w2.shape[0])).astype(BF16),
              row(a0), _pad_cols(a1, lp(a1.shape[1])).astype(BF16), _pad_rows(a2, lp(a2.shape[0])).astype(BF16),
              _pad_cols(g1, lp(g1.shape[1])).astype(BF16), _pad_rows(g2, lp(g2.shape[0])).astype(BF16),
              row(k_k), row(k_a), row(r_k), e, et]
    tok = pl.BlockSpec((1, tm, d), lambda bi, si: (bi, si, 0))
    return pl.pallas_call(
        functools.partial(_rwkv_in_body, tm=tm),
        grid=(b, s // tm),
        in_specs=[tok] + [_const_spec(c.shape) for c in consts],
        out_specs=[tok] * 8,
        out_shape=[jax.ShapeDtypeStruct((b, s, d), F32)] * 8,
        scratch_shapes=[pltpu.VMEM((tm + SHIFT_HALO, d), F32)],
        compiler_params=_params(("parallel", "arbitrary")),
        name="rwkv_in",
    )(x, *consts)


def _rwkv_scan_body(r_ref, lw_ref, k_ref, v_ref, kk_ref, b_ref, y_ref, *, seq, heads):
    L = SCAN_CHUNK
    gk = heads * RWKV_HEAD
    gl = heads * L
    row = lax.broadcasted_iota(jnp.int32, (L, gl), 0)
    pos = lax.broadcasted_iota(jnp.int32, (L, gl), 1) % L
    strict = pos < row
    incl = pos <= row
    eye = (pos == row).astype(F32)
    bd_mask = (lax.broadcasted_iota(jnp.int32, (gl, gk), 0) // L) == (lax.broadcasted_iota(jnp.int32, (gl, gk), 1) // RWKV_HEAD)
    st_mask = (lax.broadcasted_iota(jnp.int32, (gk, gk), 0) // RWKV_HEAD) == (lax.broadcasted_iota(jnp.int32, (gk, gk), 1) // RWKV_HEAD)
    tri = (lax.broadcasted_iota(jnp.int32, (L, L), 1) <= lax.broadcasted_iota(jnp.int32, (L, L), 0)).astype(BF16)

    def bd(x):
        return jnp.where(bd_mask, jnp.concatenate([x] * heads, axis=0), 0.0).astype(BF16)

    def chunk(c, ht):
        rows = pl.ds(pl.multiple_of(c * L, L), L)
        r, lw, k, v, kk, b = (ref[0, rows, :] for ref in (r_ref, lw_ref, k_ref, v_ref, kk_ref, b_ref))
        lw_hi, lw_lo = _hilo(lw)
        cum = _dot(tri, lw_hi) + _dot(tri, lw_lo)
        e_pos = jnp.exp(cum)
        e_neg = jnp.exp(-cum)
        at = -kk * jnp.exp(cum - lw)
        bt = b * e_neg
        kt = k * e_neg
        rt = r * e_pos
        p_last = e_pos[L - 1:L, :]
        lhs = jnp.concatenate([at, rt], axis=0).astype(BF16)
        ab = _dot_nt(lhs, bd(bt))
        ak = _dot_nt(lhs, bd(kt))
        a_ab = jnp.where(strict, ab[:L], 0.0)
        a_rb = jnp.where(incl, ab[L:], 0.0)
        a_ak = jnp.where(strict, ak[:L], 0.0)
        a_rk = jnp.where(incl, ak[L:], 0.0)
        t = eye + a_ab
        npow = _dot(a_ab.astype(BF16), bd(a_ab))
        rounds = int(math.log2(L)) - 1
        for i in range(rounds):
            if i < rounds - 1:
                res = _dot(jnp.concatenate([npow, t], axis=0).astype(BF16), bd(npow))
                npow = res[:L]
                t = t + res[L:]
            else:
                t = t + _dot(t.astype(BF16), bd(npow))
        xr = _dot_nt(lhs, ht.astype(BF16))
        av = _dot(jnp.concatenate([a_ak, a_rk], axis=0).astype(BF16), bd(v))
        u = _dot(t.astype(BF16), bd(xr[:L] + av[:L]))
        y_ref[0, rows, :] = xr[L:] + av[L:] + _dot(a_rb.astype(BF16), bd(u))
        upd = _dot_tn(jnp.concatenate([u, v], axis=0).astype(BF16),
                      jnp.concatenate([bt * p_last, kt * p_last], axis=0).astype(BF16))
        return ht * p_last + jnp.where(st_mask, upd, 0.0)

    lax.fori_loop(0, seq // L, chunk, jnp.zeros((gk, gk), F32))


def _rwkv_scan(r, lw, k, v, kk, bq):
    b, s, d = r.shape
    gk = SCAN_HEADS * RWKV_HEAD
    assert s % SCAN_CHUNK == 0 and d % gk == 0
    spec = pl.BlockSpec((1, s, gk), lambda bi, gi: (bi, 0, gi))
    return pl.pallas_call(
        functools.partial(_rwkv_scan_body, seq=s, heads=SCAN_HEADS),
        grid=(b, d // gk),
        in_specs=[spec] * 6,
        out_specs=spec,
        out_shape=jax.ShapeDtypeStruct((b, s, d), F32),
        compiler_params=_params(("parallel", "parallel")),
        name="rwkv_scan",
    )(r, lw, k, v, kk, bq)


def _rwkv_out_body(x_ref, y_ref, bonus_ref, g_ref, lng_ref, lnb_ref, wo_ref, e_ref, et_ref, o_ref):
    y = y_ref[...]
    inv_n = 1.0 / RWKV_HEAD
    mu = _headsum(y, e_ref, et_ref) * inv_n
    dy = y - mu
    var = _headsum(dy * dy, e_ref, et_ref) * inv_n
    yn = dy * lax.rsqrt(var + RWKV_GN_EPS) * lng_ref[...] + lnb_ref[...]
    o_ref[...] = x_ref[...] + _dot(((yn + bonus_ref[...]) * g_ref[...]).astype(BF16), wo_ref[...])


def _rwkv_out(x2, y2, bonus2, g2, ln_g, ln_b, w_o, *, tm=512):
    t, d = x2.shape
    tm = min(tm, t)
    e, et = _head_onehot(d)
    tok = pl.BlockSpec((tm, d), lambda i: (i, 0))
    consts = [ln_g.reshape(1, d), ln_b.reshape(1, d), w_o.astype(BF16), e, et]
    return pl.pallas_call(
        _rwkv_out_body,
        grid=(t // tm,),
        in_specs=[tok] * 4 + [_const_spec(c.shape) for c in consts],
        out_specs=tok,
        out_shape=jax.ShapeDtypeStruct((t, d), F32),
        compiler_params=_params(("parallel",)),
        name="rwkv_out",
    )(x2, y2, bonus2, g2, *consts)


def kernel(x, positions, ffn_norm, ffn_w_gate, ffn_w_up, ffn_w_down, mix_norm_even, w_in, conv_w, conv_b, conv_ln_g, conv_ln_b, q_norm, w_uq, kv_norm, w_ukv, w_out, mix_norm_odd, time_mu, w_r, w_k, w_v, w_o, w0, w1, w2, a0, a1, a2, g1, g2, k_k, k_a, r_k, ln_x_g, ln_x_b, final_norm):
    b, s, d = x.shape
    depth = ffn_norm.shape[0]
    x2 = x.reshape(b * s, d)
    wg, wu, wd = ffn_w_gate.astype(BF16), ffn_w_up.astype(BF16), ffn_w_down.astype(BF16)
    for layer in range(depth):
        x2 = _ffn(x2, ffn_norm[layer, 0], wg[layer, 0], wu[layer, 0], wd[layer, 0])
        if layer % 2 == 0:
            e = layer // 2
            conv, q, k, v = _mix_in(x2.reshape(b, s, d), positions, mix_norm_even[e], w_in[e], conv_w[e], conv_b[e],
                                    conv_ln_g[e], conv_ln_b[e], q_norm[e], w_uq[e], kv_norm[e], w_ukv[e])
            attn = _attention(q, k, v)
            x2 = _mix_out(x2, conv.reshape(b * s, -1), attn.reshape(b * s, -1), w_out[e])
        else:
            o = layer // 2
            r, lw, k2, v, kk, bq, bonus, g = _rwkv_in(
                x2.reshape(b, s, d), mix_norm_odd[o], time_mu[o], w_r[o], w_k[o], w_v[o], w0[o], w1[o], w2[o],
                a0[o], a1[o], a2[o], g1[o], g2[o], k_k[o], k_a[o], r_k[o].reshape(-1))
            y = _rwkv_scan(r, lw, k2, v, kk, bq)
            x2 = _rwkv_out(x2, y.reshape(b * s, d), bonus.reshape(b * s, d), g.reshape(b * s, d),
                           ln_x_g[o], ln_x_b[o], w_o[o])
        last = layer == depth - 1
        x2 = _ffn(x2, ffn_norm[layer, 1], wg[layer, 1], wu[layer, 1], wd[layer, 1], final_norm if last else None)
    return x2.reshape(b, s, d)
```

```python
import functools
import math

import jax
import jax.numpy as jnp
from jax import lax
from jax.experimental import pallas as pl
from jax.experimental.pallas import tpu as pltpu

F32 = jnp.float32
BF16 = jnp.bfloat16

NORM_EPS = 1e-6
FFN_RES_WEIGHT = 0.5
CONV_LN_EPS = 1e-5
MLA_HEADS = 8
QK_NOPE = 128
QK_ROPE = 64
V_HEAD = 128
QK_DIM = QK_NOPE + QK_ROPE
ROPE_THETA = 10000.0
RWKV_HEAD = 64
RWKV_GN_EPS = 64e-5

LANES = 128
SUBLANES = 8
VMEM_LIMIT = 56 * 1024 * 1024
NEG_BIG = -1e30

SCAN_CHUNK = 64
SCAN_HEADS = 2


def _params(sem):
    return pltpu.CompilerParams(dimension_semantics=sem, vmem_limit_bytes=VMEM_LIMIT)


def _dot(a, b):
    return jnp.dot(a, b, preferred_element_type=F32)


def _dot_nt(a, b):
    return lax.dot_general(a, b, (((1,), (1,)), ((), ())), preferred_element_type=F32)


def _dot_tn(a, b):
    return lax.dot_general(a, b, (((0,), (0,)), ((), ())), preferred_element_type=F32)


def _rms(x, g, eps):
    ms = jnp.mean(x * x, axis=-1, keepdims=True)
    return x * lax.rsqrt(ms + eps) * g


def _hilo(x):
    hi = x.astype(BF16)
    lo = (x - hi.astype(F32)).astype(BF16)
    return hi, lo


def _const_spec(shape):
    nd = len(shape)
    return pl.BlockSpec(shape, lambda *_: (0,) * nd)


def _ffn_body(*refs, final, tf):
    if final:
        x_ref, g_ref, wg_ref, wu_ref, wd_ref, fg_ref, o_ref = refs
    else:
        x_ref, g_ref, wg_ref, wu_ref, wd_ref, o_ref = refs
    x = x_ref[...]
    h = _rms(x, g_ref[...], NORM_EPS).astype(BF16)
    acc = None
    for c0 in range(0, wg_ref.shape[1], tf):
        gate = _dot(h, wg_ref[:, c0:c0 + tf])
        up = _dot(h, wu_ref[:, c0:c0 + tf])
        act = (gate * jax.nn.sigmoid(gate) * up).astype(BF16)
        part = _dot(act, wd_ref[c0:c0 + tf, :])
        acc = part if acc is None else acc + part
    y = x + FFN_RES_WEIGHT * acc
    if final:
        y = _rms(y, fg_ref[...], NORM_EPS)
    o_ref[...] = y


def _resident_spec(shape):
    nd = len(shape)
    return pl.BlockSpec(shape, lambda *_: (0,) * nd, pipeline_mode=pl.Buffered(1))


def _ffn(x2, g, wg, wu, wd, final_g=None, *, tm=1024, tf=256):
    t, d = x2.shape
    ff = wg.shape[1]
    tm = min(tm, t)
    assert ff % tf == 0 and t % tm == 0
    final = final_g is not None
    tok = pl.BlockSpec((tm, d), lambda i: (i, 0))
    in_specs = [tok, _const_spec((1, d)), _resident_spec((d, ff)), _resident_spec((d, ff)), _resident_spec((ff, d))]
    args = [x2, g.reshape(1, d), wg, wu, wd]
    if final:
        in_specs.append(_const_spec((1, d)))
        args.append(final_g.reshape(1, d))
    return pl.pallas_call(
        functools.partial(_ffn_body, final=final, tf=tf),
        grid=(t // tm,),
        in_specs=in_specs,
        out_specs=tok,
        out_shape=jax.ShapeDtypeStruct((t, d), F32),
        compiler_params=_params(("parallel",)),
        name="ffn_final" if final else "ffn",
    )(*args)


CONV_HALO = 32
CONV_ROWS = 64


def _mix_in_body(x_ref, pos_ref, g_ref, wa_ref, wgt_ref, wql_ref, wkvl_ref, wkpe_ref, wkpes_ref,
                 cw_ref, cb_ref, lng_ref, lnb_ref, qn_ref, wqn_ref, wqp_ref, wqps_ref,
                 kvn_ref, wukv_ref, invf_ref, sgn_ref,
                 conv_ref, q_ref, k_ref, v_ref, ext_scr, acc_scr, *, tm, width):
    si = pl.program_id(1)
    d_conv = ext_scr.shape[1]
    hn = _rms(x_ref[0], g_ref[...], NORM_EPS).astype(BF16)

    za = _dot(hn, wa_ref[...])
    zg = _dot(hn, wgt_ref[...])

    @pl.when(si == 0)
    def _():
        ext_scr[0:CONV_HALO, :] = jnp.zeros((CONV_HALO, d_conv), F32)

    @pl.when(si > 0)
    def _():
        ext_scr[0:CONV_HALO, :] = ext_scr[tm:tm + CONV_HALO, :]

    ext_scr[CONV_HALO:CONV_HALO + tm, :] = za * jax.nn.sigmoid(zg)

    base = CONV_HALO - (width - 1)

    def col_block(c, carry):
        cols = pl.ds(pl.multiple_of(c * LANES, LANES), LANES)
        w = cw_ref[:, cols]
        b = cb_ref[:, cols]
        for r0 in range(0, tm, CONV_ROWS):
            acc = jnp.broadcast_to(b, (CONV_ROWS, LANES))
            for rho in range(SUBLANES):
                taps = [j for j in range(width) if (base + j) % SUBLANES == rho]
                if not taps:
                    continue
                rows = CONV_ROWS + (SUBLANES if rho else 0)
                part = None
                for j in taps:
                    off = r0 + base + j - rho
                    term = w[j:j + 1, :] * ext_scr[off:off + rows, cols]
                    part = term if part is None else part + term
                acc = acc + part[rho:rho + CONV_ROWS, :]
            acc_scr[r0:r0 + CONV_ROWS, cols] = acc
        return carry

    lax.fori_loop(0, d_conv // LANES, col_block, 0)
    hc = acc_scr[...]
    mu = jnp.mean(hc, axis=-1, keepdims=True)
    dc = hc - mu
    var = jnp.mean(dc * dc, axis=-1, keepdims=True)
    hcn = dc * lax.rsqrt(var + CONV_LN_EPS) * lng_ref[...] + lnb_ref[...]
    conv_ref[0] = (hcn * jax.nn.sigmoid(hcn)).astype(BF16)

    ang = pos_ref[0].astype(F32) * invf_ref[...]
    cos1 = jnp.cos(ang)
    sin1 = jnp.sin(ang) * sgn_ref[...]
    n_pe = MLA_HEADS * QK_ROPE
    cosq = jnp.concatenate([cos1] * (n_pe // LANES), axis=-1)
    sinq = jnp.concatenate([sin1] * (n_pe // LANES), axis=-1)

    qn = _rms(_dot(hn, wql_ref[...]), qn_ref[...], NORM_EPS).astype(BF16)
    q_nope = _dot(qn, wqn_ref[...])
    q_pe = _dot(qn, wqp_ref[...]) * cosq + _dot(qn, wqps_ref[...]) * sinq

    kvn = _rms(_dot(hn, wkvl_ref[...]), kvn_ref[...], NORM_EPS).astype(BF16)
    kv = _dot(kvn, wukv_ref[...])
    k_pe = _dot(hn, wkpe_ref[...]) * cos1 + _dot(hn, wkpes_ref[...]) * sin1
    k_pe = k_pe[:, 0:QK_ROPE].astype(BF16)

    for h in range(MLA_HEADS):
        q_ref[0, h, :, 0:QK_NOPE] = q_nope[:, h * QK_NOPE:(h + 1) * QK_NOPE].astype(BF16)
        q_ref[0, h, :, QK_NOPE:QK_DIM] = q_pe[:, h * QK_ROPE:(h + 1) * QK_ROPE].astype(BF16)
        c0 = h * (QK_NOPE + V_HEAD)
        k_ref[0, h, :, 0:QK_NOPE] = kv[:, c0:c0 + QK_NOPE].astype(BF16)
        k_ref[0, h, :, QK_NOPE:QK_DIM] = k_pe
        v_ref[0, h] = kv[:, c0 + QK_NOPE:c0 + QK_NOPE + V_HEAD].astype(BF16)


def _swap_halves(w, block):
    k, n = w.shape
    w = w.reshape(k, n // block, 2, block // 2)
    return w[:, :, ::-1, :].reshape(k, n)


def _mix_in(x, positions, norm_g, w_in, conv_w, conv_b, ln_g, ln_b, q_norm, w_uq, kv_norm, w_ukv, *, tm=512):
    b, s, d = x.shape
    tm = min(tm, s)
    width, d_conv = conv_w.shape
    q_lora = q_norm.shape[0]
    kv_lora = kv_norm.shape[0]
    assert width - 1 <= CONV_HALO and tm % CONV_ROWS == 0 and tm >= CONV_HALO
    o1, o2, o3 = 2 * d_conv, 2 * d_conv + q_lora, 2 * d_conv + q_lora + kv_lora
    w_a, w_gt = w_in[:, :d_conv].astype(BF16), w_in[:, d_conv:o1].astype(BF16)
    w_ql, w_kvl = w_in[:, o1:o2].astype(BF16), w_in[:, o2:o3].astype(BF16)
    w_kpe = w_in[:, o3:]
    pad = jnp.zeros((d, LANES - QK_ROPE), F32)
    w_kpes = jnp.concatenate([_swap_halves(w_kpe, QK_ROPE), pad], axis=1).astype(BF16)
    w_kpe = jnp.concatenate([w_kpe, pad], axis=1).astype(BF16)
    wq = w_uq.reshape(q_lora, MLA_HEADS, QK_DIM)
    w_qn = wq[:, :, :QK_NOPE].reshape(q_lora, MLA_HEADS * QK_NOPE).astype(BF16)
    w_qp = wq[:, :, QK_NOPE:].reshape(q_lora, MLA_HEADS * QK_ROPE)
    w_qps = _swap_halves(w_qp, QK_ROPE).astype(BF16)
    w_qp = w_qp.astype(BF16)
    inv_freq = 1.0 / (ROPE_THETA ** (jnp.arange(0, QK_ROPE, 2, dtype=F32) / QK_ROPE))
    invf = jnp.tile(inv_freq, 2 * LANES // QK_ROPE).reshape(1, LANES)
    half = QK_ROPE // 2
    sgn = jnp.tile(jnp.concatenate([-jnp.ones((half,), F32), jnp.ones((half,), F32)]), LANES // QK_ROPE).reshape(1, LANES)

    row = lambda v: v.reshape(1, -1)
    consts = [row(norm_g), w_a, w_gt, w_ql, w_kvl, w_kpe, w_kpes, conv_w, row(conv_b), row(ln_g), row(ln_b),
              row(q_norm), w_qn, w_qp, w_qps, row(kv_norm), w_ukv.astype(BF16), invf, sgn]
    in_specs = [pl.BlockSpec((1, tm, d), lambda bi, si: (bi, si, 0)),
                pl.BlockSpec((1, tm, 1), lambda bi, si: (bi, si, 0))]
    in_specs += [_const_spec(c.shape) for c in consts]
    hb = lambda bi, si: (bi, 0, si, 0)
    return pl.pallas_call(
        functools.partial(_mix_in_body, tm=tm, width=width),
        grid=(b, s // tm),
        in_specs=in_specs,
        out_specs=[pl.BlockSpec((1, tm, d_conv), lambda bi, si: (bi, si, 0)),
                   pl.BlockSpec((1, MLA_HEADS, tm, QK_DIM), hb),
                   pl.BlockSpec((1, MLA_HEADS, tm, QK_DIM), hb),
                   pl.BlockSpec((1, MLA_HEADS, tm, V_HEAD), hb)],
        out_shape=[jax.ShapeDtypeStruct((b, s, d_conv), BF16),
                   jax.ShapeDtypeStruct((b, MLA_HEADS, s, QK_DIM), BF16),
                   jax.ShapeDtypeStruct((b, MLA_HEADS, s, QK_DIM), BF16),
                   jax.ShapeDtypeStruct((b, MLA_HEADS, s, V_HEAD), BF16)],
        scratch_shapes=[pltpu.VMEM((tm + CONV_HALO, d_conv), F32), pltpu.VMEM((tm, d_conv), F32)],
        compiler_params=_params(("parallel", "arbitrary")),
        name="mix_in",
    )(x, positions.reshape(b, s, 1), *consts)


def _attn_body(q_ref, k_ref, v_ref, o_ref, *, tq, exp2_scale):
    qi = pl.program_id(2)
    q = q_ref[0, 0]

    def block(j, carry, diagonal):
        m, l, acc = carry
        rows = pl.ds(pl.multiple_of(j * tq, tq), tq)
        s = _dot_nt(q, k_ref[0, 0, rows, :])
        if diagonal:
            r = lax.broadcasted_iota(jnp.int32, (tq, tq), 0)
            c = lax.broadcasted_iota(jnp.int32, (tq, tq), 1)
            s = jnp.where(r >= c, s, NEG_BIG)
        m_new = jnp.maximum(m, jnp.max(s, axis=-1, keepdims=True))
        p = jnp.exp2((s - m_new) * exp2_scale)
        alpha = jnp.exp2((m - m_new) * exp2_scale)
        l = alpha * l + jnp.sum(p, axis=-1, keepdims=True)
        acc = alpha * acc + _dot(p.astype(BF16), v_ref[0, 0, rows, :])
        return m_new, l, acc

    init = (jnp.full((tq, 1), NEG_BIG, F32), jnp.zeros((tq, 1), F32), jnp.zeros((tq, V_HEAD), F32))
    carry = lax.fori_loop(0, qi, lambda j, c: block(j, c, False), init)
    _, l, acc = block(qi, carry, True)
    o_ref[0] = (acc / l).astype(BF16)


def _attention(q, k, v, *, tq=512):
    b, h, s, _ = q.shape
    tq = min(tq, s)
    exp2_scale = (QK_DIM ** -0.5) * math.log2(math.e)
    return pl.pallas_call(
        functools.partial(_attn_body, tq=tq, exp2_scale=exp2_scale),
        grid=(b, h, s // tq),
        in_specs=[pl.BlockSpec((1, 1, tq, QK_DIM), lambda bi, hi, qi: (bi, hi, qi, 0)),
                  pl.BlockSpec((1, 1, s, QK_DIM), lambda bi, hi, qi: (bi, hi, 0, 0)),
                  pl.BlockSpec((1, 1, s, V_HEAD), lambda bi, hi, qi: (bi, hi, 0, 0))],
        out_specs=pl.BlockSpec((1, tq, V_HEAD), lambda bi, hi, qi: (bi, qi, hi)),
        out_shape=jax.ShapeDtypeStruct((b, s, h * V_HEAD), BF16),
        compiler_params=_params(("parallel", "parallel", "arbitrary")),
        name="attention",
    )(q, k, v)


def _mix_out_body(x_ref, c_ref, a_ref, wc_ref, wa_ref, o_ref):
    o_ref[...] = x_ref[...] + _dot(c_ref[...], wc_ref[...]) + _dot(a_ref[...], wa_ref[...])


def _mix_out(x2, conv2, attn2, w_out, *, tm=1024):
    t, d = x2.shape
    tm = min(tm, t)
    dc, da = conv2.shape[1], attn2.shape[1]
    wc, wa = w_out[:dc].astype(BF16), w_out[dc:].astype(BF16)
    return pl.pallas_call(
        _mix_out_body,
        grid=(t // tm,),
        in_specs=[pl.BlockSpec((tm, d), lambda i: (i, 0)),
                  pl.BlockSpec((tm, dc), lambda i: (i, 0)),
                  pl.BlockSpec((tm, da), lambda i: (i, 0)),
                  _const_spec(wc.shape), _const_spec(wa.shape)],
        out_specs=pl.BlockSpec((tm, d), lambda i: (i, 0)),
        out_shape=jax.ShapeDtypeStruct((t, d), F32),
        compiler_params=_params(("parallel",)),
        name="mix_out",
    )(x2, conv2, attn2, wc, wa)


SHIFT_HALO = 8


def _headsum(x, e_ref, et_ref):
    hi, lo = _hilo(x)
    s = _dot(hi, e_ref[...]) + _dot(lo, e_ref[...])
    shi, slo = _hilo(s)
    return _dot(shi, et_ref[...]) + _dot(slo, et_ref[...])


def _rwkv_in_body(x_ref, g_ref, mu_ref, wr_ref, wk_ref, wv_ref, w0_ref, w1_ref, w2_ref, a0_ref, a1_ref, a2_ref,
                  g1_ref, g2_ref, kk_ref, ka_ref, rk_ref, e_ref, et_ref,
                  r_out, lw_out, k_out, v_out, kk_out, b_out, bonus_out, g_out, ext_scr, *, tm):
    si = pl.program_id(1)
    d = ext_scr.shape[1]
    h = _rms(x_ref[0], g_ref[...], NORM_EPS)

    @pl.when(si == 0)
    def _():
        ext_scr[0:SHIFT_HALO, :] = jnp.zeros((SHIFT_HALO, d), F32)

    @pl.when(si > 0)
    def _():
        ext_scr[0:SHIFT_HALO, :] = ext_scr[tm:tm + SHIFT_HALO, :]

    ext_scr[SHIFT_HALO:SHIFT_HALO + tm, :] = h
    hh = ext_scr[SHIFT_HALO - 1:SHIFT_HALO - 1 + tm, :] - h
    mix = lambda i: (h + hh * mu_ref[i:i + 1, :]).astype(BF16)
    xr, xw, xk, xv, xa, xg = [mix(i) for i in range(6)]
    r = _dot(xr, wr_ref[...])
    k = _dot(xk, wk_ref[...])
    v = _dot(xv, wv_ref[...])
    z = w0_ref[...] + _dot(jnp.tanh(_dot(xw, w1_ref[...])).astype(BF16), w2_ref[...])
    lw = -math.exp(-0.5) * jax.nn.sigmoid(z)
    a = jax.nn.sigmoid(a0_ref[...] + _dot(_dot(xa, a1_ref[...]).astype(BF16), a2_ref[...]))
    g = _dot(jax.nn.sigmoid(_dot(xg, g1_ref[...])).astype(BF16), g2_ref[...])
    kkr = k * kk_ref[...]
    kk = kkr / jnp.maximum(jnp.sqrt(_headsum(kkr * kkr, e_ref, et_ref)), 1e-12)
    k2 = k * (1.0 + (a - 1.0) * ka_ref[...])
    r_out[0] = r
    lw_out[0] = lw
    k_out[0] = k2
    v_out[0] = v
    kk_out[0] = kk
    b_out[0] = kk * a
    bonus_out[0] = _headsum(r * k2 * rk_ref[...], e_ref, et_ref) * v
    g_out[0] = g


def _pad_cols(w, n):
    return jnp.pad(w, ((0, 0), (0, n - w.shape[1])))


def _pad_rows(w, n):
    return jnp.pad(w, ((0, n - w.shape[0]), (0, 0)))


def _head_onehot(d):
    heads = d // RWKV_HEAD
    e = (jnp.arange(d)[:, None] // RWKV_HEAD == jnp.arange(LANES)[None, :]).astype(BF16)
    assert heads <= LANES
    return e, e.T


def _rwkv_in(x, norm_g, time_mu, w_r, w_k, w_v, w0, w1, w2, a0, a1, a2, g1, g2, k_k, k_a, r_k, *, tm=256):
    b, s, d = x.shape
    tm = min(tm, s)
    row = lambda v: v.reshape(1, -1)
    lp = lambda n: ((n + LANES - 1) // LANES) * LANES
    e, et = _head_onehot(d)
    consts = [row(norm_g), time_mu, w_r.astype(BF16), w_k.astype(BF16), w_v.astype(BF16),
              row(w0), _pad_cols(w1, lp(w1.shape[1])).astype(BF16), _pad_rows(w2, lp(w2.shape[0])).astype(BF16),
              row(a0), _pad_cols(a1, lp(a1.shape[1])).astype(BF16), _pad_rows(a2, lp(a2.shape[0])).astype(BF16),
              _pad_cols(g1, lp(g1.shape[1])).astype(BF16), _pad_rows(g2, lp(g2.shape[0])).astype(BF16),
              row(k_k), row(k_a), row(r_k), e, et]
    tok = pl.BlockSpec((1, tm, d), lambda bi, si: (bi, si, 0))
    return pl.pallas_call(
        functools.partial(_rwkv_in_body, tm=tm),
        grid=(b, s // tm),
        in_specs=[tok] + [_const_spec(c.shape) for c in consts],
        out_specs=[tok] * 8,
        out_shape=[jax.ShapeDtypeStruct((b, s, d), F32)] * 8,
        scratch_shapes=[pltpu.VMEM((tm + SHIFT_HALO, d), F32)],
        compiler_params=_params(("parallel", "arbitrary")),
        name="rwkv_in",
    )(x, *consts)


def _rwkv_scan_body(r_ref, lw_ref, k_ref, v_ref, kk_ref, b_ref, y_ref, ht_scr, *, tile, heads, unroll):
    L = SCAN_CHUNK
    gk = heads * RWKV_HEAD
    gl = heads * L
    groups = ht_scr.shape[0]
    row = lax.broadcasted_iota(jnp.int32, (L, gl), 0)
    pos = lax.broadcasted_iota(jnp.int32, (L, gl), 1) % L
    strict = pos < row
    incl = pos <= row
    eye = (pos == row).astype(F32)
    bd_mask = (lax.broadcasted_iota(jnp.int32, (gl, gk), 0) // L) == (lax.broadcasted_iota(jnp.int32, (gl, gk), 1) // RWKV_HEAD)
    st_mask = (lax.broadcasted_iota(jnp.int32, (gk, gk), 0) // RWKV_HEAD) == (lax.broadcasted_iota(jnp.int32, (gk, gk), 1) // RWKV_HEAD)
    tri = (lax.broadcasted_iota(jnp.int32, (L, L), 1) <= lax.broadcasted_iota(jnp.int32, (L, L), 0)).astype(BF16)

    @pl.when(pl.program_id(1) == 0)
    def _():
        ht_scr[...] = jnp.zeros_like(ht_scr)

    def bd(x):
        return jnp.where(bd_mask, jnp.concatenate([x] * heads, axis=0), 0.0).astype(BF16)

    def chunk(c, carry):
        rows = pl.ds(pl.multiple_of(c * L, L), L)
        gs = range(groups)
        each = lambda f, *xs: [f(*a) for a in zip(*xs)]
        cat = lambda x, y: jnp.concatenate([x, y], axis=0)
        load = lambda ref: [ref[0, rows, g * gk:(g + 1) * gk] for g in gs]
        r, lw, k, v, kk, b = (load(ref) for ref in (r_ref, lw_ref, k_ref, v_ref, kk_ref, b_ref))
        ht = [ht_scr[g] for g in gs]
        cum = each(lambda x: _dot(tri, _hilo(x)[0]) + _dot(tri, _hilo(x)[1]), lw)
        e_pos = each(jnp.exp, cum)
        e_neg = each(lambda x: jnp.exp(-x), cum)
        at = each(lambda kk_, c_, lw_: -kk_ * jnp.exp(c_ - lw_), kk, cum, lw)
        bt = each(jnp.multiply, b, e_neg)
        kt = each(jnp.multiply, k, e_neg)
        rt = each(jnp.multiply, r, e_pos)
        p_last = [e[L - 1:L, :] for e in e_pos]
        lhs = each(lambda x, y: cat(x, y).astype(BF16), at, rt)
        ab = each(lambda l_, x: _dot_nt(l_, bd(x)), lhs, bt)
        ak = each(lambda l_, x: _dot_nt(l_, bd(x)), lhs, kt)
        xr = each(lambda l_, h_: _dot_nt(l_, h_.astype(BF16)), lhs, ht)
        a_ab = [jnp.where(strict, x[:L], 0.0) for x in ab]
        a_rb = [jnp.where(incl, x[L:], 0.0) for x in ab]
        av = each(lambda x, v_: _dot(cat(jnp.where(strict, x[:L], 0.0), jnp.where(incl, x[L:], 0.0)).astype(BF16), bd(v_)),
                  ak, v)
        t = [eye + x for x in a_ab]
        npow = each(lambda x: _dot(x.astype(BF16), bd(x)), a_ab)
        rounds = int(math.log2(L)) - 1
        for i in range(rounds - 1):
            res = each(lambda p_, t_: _dot(cat(p_, t_).astype(BF16), bd(p_)), npow, t)
            npow = [x[:L] for x in res]
            t = each(lambda t_, x: t_ + x[L:], t, res)
        t = each(lambda t_, p_: t_ + _dot(t_.astype(BF16), bd(p_)), t, npow)
        u = each(lambda t_, x, a_: _dot(t_.astype(BF16), bd(x[:L] + a_[:L])), t, xr, av)
        yu = each(lambda a_, u_: _dot(a_.astype(BF16), bd(u_)), a_rb, u)
        upd = each(lambda u_, v_, b_, k_, p_: _dot_tn(cat(u_, v_).astype(BF16), cat(b_ * p_, k_ * p_).astype(BF16)),
                   u, v, bt, kt, p_last)
        for g in gs:
            y_ref[0, rows, g * gk:(g + 1) * gk] = xr[g][L:] + av[g][L:] + yu[g]
            ht_scr[g] = ht[g] * p_last[g] + jnp.where(st_mask, upd[g], 0.0)
        return carry

    lax.fori_loop(0, tile // L, chunk, 0, unroll=unroll)


def _rwkv_scan(r, lw, k, v, kk, bq, *, tile=512, unroll=2):
    b, s, d = r.shape
    gk = SCAN_HEADS * RWKV_HEAD
    tile = min(tile, s)
    assert SCAN_CHUNK == RWKV_HEAD and tile % SCAN_CHUNK == 0 and d % gk == 0
    spec = pl.BlockSpec((1, tile, d), lambda bi, si: (bi, si, 0))
    return pl.pallas_call(
        functools.partial(_rwkv_scan_body, tile=tile, heads=SCAN_HEADS, unroll=unroll),
        grid=(b, s // tile),
        in_specs=[spec] * 6,
        out_specs=spec,
        out_shape=jax.ShapeDtypeStruct((b, s, d), F32),
        scratch_shapes=[pltpu.VMEM((d // gk, gk, gk), F32)],
        compiler_params=_params(("parallel", "arbitrary")),
        name="rwkv_scan",
    )(r, lw, k, v, kk, bq)


def _rwkv_out_body(x_ref, y_ref, bonus_ref, g_ref, lng_ref, lnb_ref, wo_ref, e_ref, et_ref, o_ref):
    y = y_ref[...]
    inv_n = 1.0 / RWKV_HEAD
    mu = _headsum(y, e_ref, et_ref) * inv_n
    dy = y - mu
    var = _headsum(dy * dy, e_ref, et_ref) * inv_n
    yn = dy * lax.rsqrt(var + RWKV_GN_EPS) * lng_ref[...] + lnb_ref[...]
    o_ref[...] = x_ref[...] + _dot(((yn + bonus_ref[...]) * g_ref[...]).astype(BF16), wo_ref[...])


def _rwkv_out(x2, y2, bonus2, g2, ln_g, ln_b, w_o, *, tm=512):
    t, d = x2.shape
    tm = min(tm, t)
    e, et = _head_onehot(d)
    tok = pl.BlockSpec((tm, d), lambda i: (i, 0))
    consts = [ln_g.reshape(1, d), ln_b.reshape(1, d), w_o.astype(BF16), e, et]
    return pl.pallas_call(
        _rwkv_out_body,
        grid=(t // tm,),
        in_specs=[tok] * 4 + [_const_spec(c.shape) for c in consts],
        out_specs=tok,
        out_shape=jax.ShapeDtypeStruct((t, d), F32),
        compiler_params=_params(("parallel",)),
        name="rwkv_out",
    )(x2, y2, bonus2, g2, *consts)


def kernel(x, positions, ffn_norm, ffn_w_gate, ffn_w_up, ffn_w_down, mix_norm_even, w_in, conv_w, conv_b, conv_ln_g, conv_ln_b, q_norm, w_uq, kv_norm, w_ukv, w_out, mix_norm_odd, time_mu, w_r, w_k, w_v, w_o, w0, w1, w2, a0, a1, a2, g1, g2, k_k, k_a, r_k, ln_x_g, ln_x_b, final_norm):
    b, s, d = x.shape
    depth = ffn_norm.shape[0]
    x2 = x.reshape(b * s, d)
    wg, wu, wd = ffn_w_gate.astype(BF16), ffn_w_up.astype(BF16), ffn_w_down.astype(BF16)
    for layer in range(depth):
        x2 = _ffn(x2, ffn_norm[layer, 0], wg[layer, 0], wu[layer, 0], wd[layer, 0])
        if layer % 2 == 0:
            e = layer // 2
            conv, q, k, v = _mix_in(x2.reshape(b, s, d), positions, mix_norm_even[e], w_in[e], conv_w[e], conv_b[e],
                                    conv_ln_g[e], conv_ln_b[e], q_norm[e], w_uq[e], kv_norm[e], w_ukv[e])
            attn = _attention(q, k, v)
            x2 = _mix_out(x2, conv.reshape(b * s, -1), attn.reshape(b * s, -1), w_out[e])
        else:
            o = layer // 2
            r, lw, k2, v, kk, bq, bonus, g = _rwkv_in(
                x2.reshape(b, s, d), mix_norm_odd[o], time_mu[o], w_r[o], w_k[o], w_v[o], w0[o], w1[o], w2[o],
                a0[o], a1[o], a2[o], g1[o], g2[o], k_k[o], k_a[o], r_k[o].reshape(-1))
            y = _rwkv_scan(r, lw, k2, v, kk, bq)
            x2 = _rwkv_out(x2, y.reshape(b * s, d), bonus.reshape(b * s, d), g.reshape(b * s, d),
                           ln_x_g[o], ln_x_b[o], w_o[o])
        last = layer == depth - 1
        x2 = _ffn(x2, ffn_norm[layer, 1], wg[layer, 1], wu[layer, 1], wd[layer, 1], final_norm if last else None)
    return x2.reshape(b, s, d)
```

```python
import functools
import math

import jax
import jax.numpy as jnp
from jax import lax
from jax.experimental import pallas as pl
from jax.experimental.pallas import tpu as pltpu

F32 = jnp.float32
BF16 = jnp.bfloat16

NORM_EPS = 1e-6
FFN_RES_WEIGHT = 0.5
CONV_LN_EPS = 1e-5
MLA_HEADS = 8
QK_NOPE = 128
QK_ROPE = 64
V_HEAD = 128
QK_DIM = QK_NOPE + QK_ROPE
ROPE_THETA = 10000.0
RWKV_HEAD = 64
RWKV_GN_EPS = 64e-5

LANES = 128
SUBLANES = 8
VMEM_LIMIT = 56 * 1024 * 1024
NEG_BIG = -1e30

SCAN_CHUNK = 64
SCAN_HEADS = 2


def _params(sem):
    return pltpu.CompilerParams(dimension_semantics=sem, vmem_limit_bytes=VMEM_LIMIT)


def _dot(a, b):
    return jnp.dot(a, b, preferred_element_type=F32)


def _dot_nt(a, b):
    return lax.dot_general(a, b, (((1,), (1,)), ((), ())), preferred_element_type=F32)


def _dot_tn(a, b):
    return lax.dot_general(a, b, (((0,), (0,)), ((), ())), preferred_element_type=F32)


def _rms(x, g, eps):
    ms = jnp.mean(x * x, axis=-1, keepdims=True)
    return x * lax.rsqrt(ms + eps) * g


def _hilo(x):
    hi = x.astype(BF16)
    lo = (x - hi.astype(F32)).astype(BF16)
    return hi, lo


def _const_spec(shape):
    nd = len(shape)
    return pl.BlockSpec(shape, lambda *_: (0,) * nd)


def _ffn_body(*refs, final, tf):
    if final:
        x_ref, g_ref, wg_ref, wu_ref, wd_ref, fg_ref, o_ref = refs
    else:
        x_ref, g_ref, wg_ref, wu_ref, wd_ref, o_ref = refs
    x = x_ref[...]
    h = _rms(x, g_ref[...], NORM_EPS).astype(BF16)
    acc = None
    for c0 in range(0, wg_ref.shape[1], tf):
        gate = _dot(h, wg_ref[:, c0:c0 + tf])
        up = _dot(h, wu_ref[:, c0:c0 + tf])
        act = (gate * jax.nn.sigmoid(gate) * up).astype(BF16)
        part = _dot(act, wd_ref[c0:c0 + tf, :])
        acc = part if acc is None else acc + part
    y = x + FFN_RES_WEIGHT * acc
    if final:
        y = _rms(y, fg_ref[...], NORM_EPS)
    o_ref[...] = y


def _resident_spec(shape):
    nd = len(shape)
    return pl.BlockSpec(shape, lambda *_: (0,) * nd, pipeline_mode=pl.Buffered(1))


def _ffn(x2, g, wg, wu, wd, final_g=None, *, tm=1024, tf=256):
    t, d = x2.shape
    ff = wg.shape[1]
    tm = min(tm, t)
    assert ff % tf == 0 and t % tm == 0
    final = final_g is not None
    tok = pl.BlockSpec((tm, d), lambda i: (i, 0))
    in_specs = [tok, _const_spec((1, d)), _resident_spec((d, ff)), _resident_spec((d, ff)), _resident_spec((ff, d))]
    args = [x2, g.reshape(1, d), wg, wu, wd]
    if final:
        in_specs.append(_const_spec((1, d)))
        args.append(final_g.reshape(1, d))
    return pl.pallas_call(
        functools.partial(_ffn_body, final=final, tf=tf),
        grid=(t // tm,),
        in_specs=in_specs,
        out_specs=tok,
        out_shape=jax.ShapeDtypeStruct((t, d), F32),
        compiler_params=_params(("parallel",)),
        name="ffn_final" if final else "ffn",
    )(*args)


CONV_HALO = 32
CONV_ROWS = 64


def _mix_in_body(x_ref, pos_ref, g_ref, wa_ref, wgt_ref, wql_ref, wkvl_ref, wkpe_ref, wkpes_ref,
                 cw_ref, cb_ref, lng_ref, lnb_ref, qn_ref, wqn_ref, wqp_ref, wqps_ref,
                 kvn_ref, wukv_ref, invf_ref, sgn_ref,
                 conv_ref, q_ref, k_ref, v_ref, ext_scr, acc_scr, *, tm, width):
    si = pl.program_id(1)
    d_conv = ext_scr.shape[1]
    hn = _rms(x_ref[0], g_ref[...], NORM_EPS).astype(BF16)

    za = _dot(hn, wa_ref[...])
    zg = _dot(hn, wgt_ref[...])

    @pl.when(si == 0)
    def _():
        ext_scr[0:CONV_HALO, :] = jnp.zeros((CONV_HALO, d_conv), F32)

    @pl.when(si > 0)
    def _():
        ext_scr[0:CONV_HALO, :] = ext_scr[tm:tm + CONV_HALO, :]

    ext_scr[CONV_HALO:CONV_HALO + tm, :] = za * jax.nn.sigmoid(zg)

    base = CONV_HALO - (width - 1)

    def col_block(c, carry):
        cols = pl.ds(pl.multiple_of(c * LANES, LANES), LANES)
        w = cw_ref[:, cols]
        b = cb_ref[:, cols]
        for r0 in range(0, tm, CONV_ROWS):
            acc = jnp.broadcast_to(b, (CONV_ROWS, LANES))
            for rho in range(SUBLANES):
                taps = [j for j in range(width) if (base + j) % SUBLANES == rho]
                if not taps:
                    continue
                rows = CONV_ROWS + (SUBLANES if rho else 0)
                part = None
                for j in taps:
                    off = r0 + base + j - rho
                    term = w[j:j + 1, :] * ext_scr[off:off + rows, cols]
                    part = term if part is None else part + term
                acc = acc + part[rho:rho + CONV_ROWS, :]
            acc_scr[r0:r0 + CONV_ROWS, cols] = acc
        return carry

    lax.fori_loop(0, d_conv // LANES, col_block, 0)
    hc = acc_scr[...]
    mu = jnp.mean(hc, axis=-1, keepdims=True)
    dc = hc - mu
    var = jnp.mean(dc * dc, axis=-1, keepdims=True)
    hcn = dc * lax.rsqrt(var + CONV_LN_EPS) * lng_ref[...] + lnb_ref[...]
    conv_ref[0] = (hcn * jax.nn.sigmoid(hcn)).astype(BF16)

    ang = pos_ref[0].astype(F32) * invf_ref[...]
    cos1 = jnp.cos(ang)
    sin1 = jnp.sin(ang) * sgn_ref[...]
    n_pe = MLA_HEADS * QK_ROPE
    cosq = jnp.concatenate([cos1] * (n_pe // LANES), axis=-1)
    sinq = jnp.concatenate([sin1] * (n_pe // LANES), axis=-1)

    qn = _rms(_dot(hn, wql_ref[...]), qn_ref[...], NORM_EPS).astype(BF16)
    q_nope = _dot(qn, wqn_ref[...])
    q_pe = _dot(qn, wqp_ref[...]) * cosq + _dot(qn, wqps_ref[...]) * sinq

    kvn = _rms(_dot(hn, wkvl_ref[...]), kvn_ref[...], NORM_EPS).astype(BF16)
    kv = _dot(kvn, wukv_ref[...])
    k_pe = _dot(hn, wkpe_ref[...]) * cos1 + _dot(hn, wkpes_ref[...]) * sin1
    k_pe = k_pe[:, 0:QK_ROPE].astype(BF16)

    for h in range(MLA_HEADS):
        q_ref[0, h, :, 0:QK_NOPE] = q_nope[:, h * QK_NOPE:(h + 1) * QK_NOPE].astype(BF16)
        q_ref[0, h, :, QK_NOPE:QK_DIM] = q_pe[:, h * QK_ROPE:(h + 1) * QK_ROPE].astype(BF16)
        c0 = h * (QK_NOPE + V_HEAD)
        k_ref[0, h, :, 0:QK_NOPE] = kv[:, c0:c0 + QK_NOPE].astype(BF16)
        k_ref[0, h, :, QK_NOPE:QK_DIM] = k_pe
        v_ref[0, h] = kv[:, c0 + QK_NOPE:c0 + QK_NOPE + V_HEAD].astype(BF16)


def _swap_halves(w, block):
    k, n = w.shape
    w = w.reshape(k, n // block, 2, block // 2)
    return w[:, :, ::-1, :].reshape(k, n)


def _mix_in(x, positions, norm_g, w_in, conv_w, conv_b, ln_g, ln_b, q_norm, w_uq, kv_norm, w_ukv, *, tm=512):
    b, s, d = x.shape
    tm = min(tm, s)
    width, d_conv = conv_w.shape
    q_lora = q_norm.shape[0]
    kv_lora = kv_norm.shape[0]
    assert width - 1 <= CONV_HALO and tm % CONV_ROWS == 0 and tm >= CONV_HALO
    o1, o2, o3 = 2 * d_conv, 2 * d_conv + q_lora, 2 * d_conv + q_lora + kv_lora
    w_a, w_gt = w_in[:, :d_conv].astype(BF16), w_in[:, d_conv:o1].astype(BF16)
    w_ql, w_kvl = w_in[:, o1:o2].astype(BF16), w_in[:, o2:o3].astype(BF16)
    w_kpe = w_in[:, o3:]
    pad = jnp.zeros((d, LANES - QK_ROPE), F32)
    w_kpes = jnp.concatenate([_swap_halves(w_kpe, QK_ROPE), pad], axis=1).astype(BF16)
    w_kpe = jnp.concatenate([w_kpe, pad], axis=1).astype(BF16)
    wq = w_uq.reshape(q_lora, MLA_HEADS, QK_DIM)
    w_qn = wq[:, :, :QK_NOPE].reshape(q_lora, MLA_HEADS * QK_NOPE).astype(BF16)
    w_qp = wq[:, :, QK_NOPE:].reshape(q_lora, MLA_HEADS * QK_ROPE)
    w_qps = _swap_halves(w_qp, QK_ROPE).astype(BF16)
    w_qp = w_qp.astype(BF16)
    inv_freq = 1.0 / (ROPE_THETA ** (jnp.arange(0, QK_ROPE, 2, dtype=F32) / QK_ROPE))
    invf = jnp.tile(inv_freq, 2 * LANES // QK_ROPE).reshape(1, LANES)
    half = QK_ROPE // 2
    sgn = jnp.tile(jnp.concatenate([-jnp.ones((half,), F32), jnp.ones((half,), F32)]), LANES // QK_ROPE).reshape(1, LANES)

    row = lambda v: v.reshape(1, -1)
    consts = [row(norm_g), w_a, w_gt, w_ql, w_kvl, w_kpe, w_kpes, conv_w, row(conv_b), row(ln_g), row(ln_b),
              row(q_norm), w_qn, w_qp, w_qps, row(kv_norm), w_ukv.astype(BF16), invf, sgn]
    in_specs = [pl.BlockSpec((1, tm, d), lambda bi, si: (bi, si, 0)),
                pl.BlockSpec((1, tm, 1), lambda bi, si: (bi, si, 0))]
    in_specs += [_const_spec(c.shape) for c in consts]
    hb = lambda bi, si: (bi, 0, si, 0)
    return pl.pallas_call(
        functools.partial(_mix_in_body, tm=tm, width=width),
        grid=(b, s // tm),
        in_specs=in_specs,
        out_specs=[pl.BlockSpec((1, tm, d_conv), lambda bi, si: (bi, si, 0)),
                   pl.BlockSpec((1, MLA_HEADS, tm, QK_DIM), hb),
                   pl.BlockSpec((1, MLA_HEADS, tm, QK_DIM), hb),
                   pl.BlockSpec((1, MLA_HEADS, tm, V_HEAD), hb)],
        out_shape=[jax.ShapeDtypeStruct((b, s, d_conv), BF16),
                   jax.ShapeDtypeStruct((b, MLA_HEADS, s, QK_DIM), BF16),
                   jax.ShapeDtypeStruct((b, MLA_HEADS, s, QK_DIM), BF16),
                   jax.ShapeDtypeStruct((b, MLA_HEADS, s, V_HEAD), BF16)],
        scratch_shapes=[pltpu.VMEM((tm + CONV_HALO, d_conv), F32), pltpu.VMEM((tm, d_conv), F32)],
        compiler_params=_params(("parallel", "arbitrary")),
        name="mix_in",
    )(x, positions.reshape(b, s, 1), *consts)


def _attn_body(q_ref, k_ref, v_ref, o_ref, *, tq, exp2_scale):
    seq = q_ref.shape[2]
    rows = lambda i: slice(i * tq, (i + 1) * tq)
    r = lax.broadcasted_iota(jnp.int32, (tq, tq), 0)
    c = lax.broadcasted_iota(jnp.int32, (tq, tq), 1)
    causal = r >= c

    def scores(qi, j):
        return _dot_nt(q_ref[0, 0, rows(qi), :], k_ref[0, 0, rows(j), :])

    pairs = [(qi, j) for qi in range(seq // tq) for j in range(qi + 1)]
    s_next = scores(*pairs[0])
    for idx, (qi, j) in enumerate(pairs):
        s = s_next
        if idx + 1 < len(pairs):
            s_next = scores(*pairs[idx + 1])
        if j == 0:
            m = jnp.full((tq, 1), NEG_BIG, F32)
            l = jnp.zeros((tq, 1), F32)
            acc = jnp.zeros((tq, V_HEAD), F32)
        if j == qi:
            s = jnp.where(causal, s, NEG_BIG)
        m_new = jnp.maximum(m, jnp.max(s, axis=-1, keepdims=True))
        p = jnp.exp2((s - m_new) * exp2_scale)
        alpha = jnp.exp2((m - m_new) * exp2_scale)
        l = alpha * l + jnp.sum(p, axis=-1, keepdims=True)
        acc = alpha * acc + _dot(p.astype(BF16), v_ref[0, 0, rows(j), :])
        m = m_new
        if j == qi:
            o_ref[0, rows(qi), :] = (acc / l).astype(BF16)


def _attention(q, k, v, *, tq=512):
    b, h, s, _ = q.shape
    tq = min(tq, s)
    assert s % tq == 0
    exp2_scale = (QK_DIM ** -0.5) * math.log2(math.e)
    head = lambda d: pl.BlockSpec((1, 1, s, d), lambda bi, hi: (bi, hi, 0, 0))
    return pl.pallas_call(
        functools.partial(_attn_body, tq=tq, exp2_scale=exp2_scale),
        grid=(b, h),
        in_specs=[head(QK_DIM), head(QK_DIM), head(V_HEAD)],
        out_specs=pl.BlockSpec((1, s, V_HEAD), lambda bi, hi: (bi, 0, hi)),
        out_shape=jax.ShapeDtypeStruct((b, s, h * V_HEAD), BF16),
        compiler_params=_params(("parallel", "parallel")),
        name="attention",
    )(q, k, v)


def _mix_out_body(x_ref, c_ref, a_ref, wc_ref, wa_ref, o_ref):
    o_ref[...] = x_ref[...] + _dot(c_ref[...], wc_ref[...]) + _dot(a_ref[...], wa_ref[...])


def _mix_out(x2, conv2, attn2, w_out, *, tm=1024):
    t, d = x2.shape
    tm = min(tm, t)
    dc, da = conv2.shape[1], attn2.shape[1]
    wc, wa = w_out[:dc].astype(BF16), w_out[dc:].astype(BF16)
    return pl.pallas_call(
        _mix_out_body,
        grid=(t // tm,),
        in_specs=[pl.BlockSpec((tm, d), lambda i: (i, 0)),
                  pl.BlockSpec((tm, dc), lambda i: (i, 0)),
                  pl.BlockSpec((tm, da), lambda i: (i, 0)),
                  _const_spec(wc.shape), _const_spec(wa.shape)],
        out_specs=pl.BlockSpec((tm, d), lambda i: (i, 0)),
        out_shape=jax.ShapeDtypeStruct((t, d), F32),
        compiler_params=_params(("parallel",)),
        name="mix_out",
    )(x2, conv2, attn2, wc, wa)


SHIFT_HALO = 8


def _headsum(x, e_ref, et_ref):
    hi, lo = _hilo(x)
    s = _dot(hi, e_ref[...]) + _dot(lo, e_ref[...])
    shi, slo = _hilo(s)
    return _dot(shi, et_ref[...]) + _dot(slo, et_ref[...])


def _rwkv_in_body(x_ref, g_ref, mu_ref, wr_ref, wk_ref, wv_ref, w0_ref, w1_ref, w2_ref, a0_ref, a1_ref, a2_ref,
                  g1_ref, g2_ref, kk_ref, ka_ref, rk_ref, e_ref, et_ref, tri_ref,
                  ar_out, bk_out, v_out, pl_out, bonus_out, g_out, carry_scr, *, tm):
    si = pl.program_id(1)
    L = SCAN_CHUNK
    d = carry_scr.shape[1]
    h = _rms(x_ref[0], g_ref[...], NORM_EPS)

    @pl.when(si == 0)
    def _():
        carry_scr[...] = jnp.zeros_like(carry_scr)

    prev_last = carry_scr[SHIFT_HALO - 1:SHIFT_HALO, :]
    first_row = lax.broadcasted_iota(jnp.int32, (tm, d), 0) == 0
    hh = jnp.where(first_row, prev_last, pltpu.roll(h, 1, axis=0)) - h
    carry_scr[...] = h[tm - SHIFT_HALO:tm, :]
    mix = lambda i: (h + hh * mu_ref[i:i + 1, :]).astype(BF16)
    xr, xw, xk, xv, xa, xg = [mix(i) for i in range(6)]
    r = _dot(xr, wr_ref[...])
    k = _dot(xk, wk_ref[...])
    v = _dot(xv, wv_ref[...])
    z = w0_ref[...] + _dot(jnp.tanh(_dot(xw, w1_ref[...])).astype(BF16), w2_ref[...])
    lw = -math.exp(-0.5) * jax.nn.sigmoid(z)
    a = jax.nn.sigmoid(a0_ref[...] + _dot(_dot(xa, a1_ref[...]).astype(BF16), a2_ref[...]))
    g = _dot(jax.nn.sigmoid(_dot(xg, g1_ref[...])).astype(BF16), g2_ref[...])
    kkr = k * kk_ref[...]
    kk = kkr * lax.rsqrt(jnp.maximum(_headsum(kkr * kkr, e_ref, et_ref), 1e-24))
    k2 = k * (1.0 + (a - 1.0) * ka_ref[...])
    bonus_out[0] = _headsum(r * k2 * rk_ref[...], e_ref, et_ref) * v
    g_out[0] = g
    v_out[0] = v.astype(BF16)

    lw_hi, lw_lo = _hilo(lw)
    cum = _dot(tri_ref[...], lw_hi) + _dot(tri_ref[...], lw_lo)
    e_pos = jnp.exp(cum)
    e_neg = jnp.exp(-cum)
    at = (-kk * jnp.exp(cum - lw)).astype(BF16)
    rt = (r * e_pos).astype(BF16)
    bt = (kk * a * e_neg).astype(BF16)
    kt = (k2 * e_neg).astype(BF16)
    for c in range(tm // L):
        tok = slice(c * L, (c + 1) * L)
        ar_out[0, 2 * c * L:(2 * c + 1) * L, :] = at[tok]
        ar_out[0, (2 * c + 1) * L:(2 * c + 2) * L, :] = rt[tok]
        bk_out[0, 2 * c * L:(2 * c + 1) * L, :] = bt[tok]
        bk_out[0, (2 * c + 1) * L:(2 * c + 2) * L, :] = kt[tok]
        pl_out[0, c * SUBLANES:(c + 1) * SUBLANES, :] = jnp.broadcast_to(e_pos[(c + 1) * L - 1:(c + 1) * L, :], (SUBLANES, d))


def _pad_cols(w, n):
    return jnp.pad(w, ((0, 0), (0, n - w.shape[1])))


def _pad_rows(w, n):
    return jnp.pad(w, ((0, n - w.shape[0]), (0, 0)))


def _head_onehot(d):
    heads = d // RWKV_HEAD
    e = (jnp.arange(d)[:, None] // RWKV_HEAD == jnp.arange(LANES)[None, :]).astype(BF16)
    assert heads <= LANES
    return e, e.T


def _rwkv_in(x, norm_g, time_mu, w_r, w_k, w_v, w0, w1, w2, a0, a1, a2, g1, g2, k_k, k_a, r_k, *, tm=256):
    b, s, d = x.shape
    tm = min(tm, s)
    L = SCAN_CHUNK
    assert tm % L == 0 and s % tm == 0
    row = lambda v: v.reshape(1, -1)
    lp = lambda n: ((n + LANES - 1) // LANES) * LANES
    e, et = _head_onehot(d)
    idx = jnp.arange(tm)
    tri = ((idx[:, None] // L == idx[None, :] // L) & (idx[None, :] <= idx[:, None])).astype(BF16)
    consts = [row(norm_g), time_mu, w_r.astype(BF16), w_k.astype(BF16), w_v.astype(BF16),
              row(w0), _pad_cols(w1, lp(w1.shape[1])).astype(BF16), _pad_rows(w2, lp(w2.shape[0])).astype(BF16),
              row(a0), _pad_cols(a1, lp(a1.shape[1])).astype(BF16), _pad_rows(a2, lp(a2.shape[0])).astype(BF16),
              _pad_cols(g1, lp(g1.shape[1])).astype(BF16), _pad_rows(g2, lp(g2.shape[0])).astype(BF16),
              row(k_k), row(k_a), row(r_k), e, et, tri]
    tok = lambda rows: pl.BlockSpec((1, rows, d), lambda bi, si: (bi, si, 0))
    pl_rows = tm // L * SUBLANES
    return pl.pallas_call(
        functools.partial(_rwkv_in_body, tm=tm),
        grid=(b, s // tm),
        in_specs=[tok(tm)] + [_const_spec(c.shape) for c in consts],
        out_specs=[tok(2 * tm), tok(2 * tm), tok(tm), tok(pl_rows), tok(tm), tok(tm)],
        out_shape=[jax.ShapeDtypeStruct((b, 2 * s, d), BF16), jax.ShapeDtypeStruct((b, 2 * s, d), BF16),
                   jax.ShapeDtypeStruct((b, s, d), BF16), jax.ShapeDtypeStruct((b, s // L * SUBLANES, d), F32),
                   jax.ShapeDtypeStruct((b, s, d), F32), jax.ShapeDtypeStruct((b, s, d), F32)],
        scratch_shapes=[pltpu.VMEM((SHIFT_HALO, d), F32)],
        compiler_params=_params(("parallel", "arbitrary")),
        name="rwkv_in",
    )(x, *consts)


def _rwkv_scan_body(ar_ref, bk_ref, v_ref, pl_ref, y_ref, ht_scr, *, tile, heads):
    L = SCAN_CHUNK
    gk = heads * RWKV_HEAD
    gl = heads * L
    nb, groups = ht_scr.shape[0], ht_scr.shape[1]
    chains = [(bi, g) for bi in range(nb) for g in range(groups)]
    row = lax.broadcasted_iota(jnp.int32, (L, gl), 0)
    pos = lax.broadcasted_iota(jnp.int32, (L, gl), 1) % L
    strict = pos < row
    incl = pos <= row
    eye = (pos == row).astype(F32)
    blk = lambda shape: (lax.broadcasted_iota(jnp.int32, shape, 0) // L) % heads == lax.broadcasted_iota(jnp.int32, shape, 1) // L
    bd_mask = blk((gl, gk))
    bd2_mask = blk((2 * gl, gk))

    @pl.when(pl.program_id(1) == 0)
    def _():
        ht_scr[...] = jnp.zeros_like(ht_scr)

    def bd(x):
        return jnp.where(bd_mask, jnp.concatenate([x] * heads, axis=0), 0).astype(BF16)

    def chunk(c, carry):
        each = lambda f, *xs: [f(*a) for a in zip(*xs)]
        cat = lambda *xs: jnp.concatenate(xs, axis=0)
        load = lambda ref, n: [ref[bi, pl.ds(pl.multiple_of(c * n, n), n), g * gk:(g + 1) * gk] for bi, g in chains]
        lhs = load(ar_ref, 2 * L)
        bk = load(bk_ref, 2 * L)
        v = load(v_ref, L)
        p_last = [x[0:1, :] for x in load(pl_ref, SUBLANES)]
        ht = [ht_scr[bi, g] for bi, g in chains]
        abk = each(lambda l_, x: _dot_nt(l_, jnp.where(bd2_mask, cat(*([x[:L]] * heads + [x[L:]] * heads)), 0)), lhs, bk)
        xr = each(lambda l_, h_: _dot_nt(l_, h_.astype(BF16)), lhs, ht)
        a_ab = [jnp.where(strict, x[:L, :gl], 0.0) for x in abk]
        a_rb = [jnp.where(incl, x[L:, :gl], 0.0) for x in abk]
        av = each(lambda x, v_: _dot(cat(jnp.where(strict, x[:L, gl:], 0.0), jnp.where(incl, x[L:, gl:], 0.0)).astype(BF16), bd(v_)),
                  abk, v)
        t = [eye + x for x in a_ab]
        p = each(lambda x: _dot(x.astype(BF16), bd(x)), a_ab)
        rounds = int(math.log2(L)) - 1
        for i in range(rounds - 1):
            res = each(lambda p_, t_: _dot(p_.astype(BF16), jnp.concatenate([bd(p_), bd(t_)], axis=1)), p, t)
            p = [x[:, :gl] for x in res]
            t = each(lambda t_, x: t_ + x[:, gl:], t, res)
        t = each(lambda t_, p_: t_ + _dot(p_.astype(BF16), bd(t_)), t, p)
        u = each(lambda t_, x, a_: _dot(t_.astype(BF16), bd(x[:L] + a_[:L])), t, xr, av)
        yu = each(lambda a_, u_: _dot(a_.astype(BF16), bd(u_)), a_rb, u)
        upd = each(lambda u_, v_, x: _dot_tn(cat(u_.astype(BF16), v_), x), u, v, bk)
        for i, (bi, g) in enumerate(chains):
            y_ref[bi, pl.ds(pl.multiple_of(c * L, L), L), g * gk:(g + 1) * gk] = xr[i][L:] + av[i][L:] + yu[i]
            ht_scr[bi, g] = (ht[i] + jnp.where(bd_mask, upd[i], 0.0)) * p_last[i]
        return carry

    lax.fori_loop(0, tile // L, chunk, 0)


def _rwkv_scan(ar, bk, v, p_last, *, nb=2, tile=512):
    b, s, d = v.shape
    gk = SCAN_HEADS * RWKV_HEAD
    L = SCAN_CHUNK
    tile = min(tile, s)
    nb = min(nb, b)
    assert L == RWKV_HEAD and tile % L == 0 and s % tile == 0 and d % gk == 0 and b % nb == 0
    spec = lambda rows: pl.BlockSpec((nb, rows, d), lambda bi, si: (bi, si, 0))
    return pl.pallas_call(
        functools.partial(_rwkv_scan_body, tile=tile, heads=SCAN_HEADS),
        grid=(b // nb, s // tile),
        in_specs=[spec(2 * tile), spec(2 * tile), spec(tile), spec(tile // L * SUBLANES)],
        out_specs=spec(tile),
        out_shape=jax.ShapeDtypeStruct((b, s, d), F32),
        scratch_shapes=[pltpu.VMEM((nb, d // gk, gk, gk), F32)],
        compiler_params=_params(("parallel", "arbitrary")),
        name="rwkv_scan",
    )(ar, bk, v, p_last)


def _rwkv_out_body(x_ref, y_ref, bonus_ref, g_ref, lng_ref, lnb_ref, wo_ref, e_ref, et_ref, o_ref):
    y = y_ref[...]
    inv_n = 1.0 / RWKV_HEAD
    mu = _headsum(y, e_ref, et_ref) * inv_n
    dy = y - mu
    var = _headsum(dy * dy, e_ref, et_ref) * inv_n
    yn = dy * lax.rsqrt(var + RWKV_GN_EPS) * lng_ref[...] + lnb_ref[...]
    o_ref[...] = x_ref[...] + _dot(((yn + bonus_ref[...]) * g_ref[...]).astype(BF16), wo_ref[...])


def _rwkv_out(x2, y2, bonus2, g2, ln_g, ln_b, w_o, *, tm=512):
    t, d = x2.shape
    tm = min(tm, t)
    e, et = _head_onehot(d)
    tok = pl.BlockSpec((tm, d), lambda i: (i, 0))
    consts = [ln_g.reshape(1, d), ln_b.reshape(1, d), w_o.astype(BF16), e, et]
    return pl.pallas_call(
        _rwkv_out_body,
        grid=(t // tm,),
        in_specs=[tok] * 4 + [_const_spec(c.shape) for c in consts],
        out_specs=tok,
        out_shape=jax.ShapeDtypeStruct((t, d), F32),
        compiler_params=_params(("parallel",)),
        name="rwkv_out",
    )(x2, y2, bonus2, g2, *consts)


def kernel(x, positions, ffn_norm, ffn_w_gate, ffn_w_up, ffn_w_down, mix_norm_even, w_in, conv_w, conv_b, conv_ln_g, conv_ln_b, q_norm, w_uq, kv_norm, w_ukv, w_out, mix_norm_odd, time_mu, w_r, w_k, w_v, w_o, w0, w1, w2, a0, a1, a2, g1, g2, k_k, k_a, r_k, ln_x_g, ln_x_b, final_norm):
    b, s, d = x.shape
    depth = ffn_norm.shape[0]
    x2 = x.reshape(b * s, d)
    wg, wu, wd = ffn_w_gate.astype(BF16), ffn_w_up.astype(BF16), ffn_w_down.astype(BF16)
    for layer in range(depth):
        x2 = _ffn(x2, ffn_norm[layer, 0], wg[layer, 0], wu[layer, 0], wd[layer, 0])
        if layer % 2 == 0:
            e = layer // 2
            conv, q, k, v = _mix_in(x2.reshape(b, s, d), positions, mix_norm_even[e], w_in[e], conv_w[e], conv_b[e],
                                    conv_ln_g[e], conv_ln_b[e], q_norm[e], w_uq[e], kv_norm[e], w_ukv[e])
            attn = _attention(q, k, v)
            x2 = _mix_out(x2, conv.reshape(b * s, -1), attn.reshape(b * s, -1), w_out[e])
        else:
            o = layer // 2
            ar, bk, v, p_last, bonus, g = _rwkv_in(
                x2.reshape(b, s, d), mix_norm_odd[o], time_mu[o], w_r[o], w_k[o], w_v[o], w0[o], w1[o], w2[o],
                a0[o], a1[o], a2[o], g1[o], g2[o], k_k[o], k_a[o], r_k[o].reshape(-1))
            y = _rwkv_scan(ar, bk, v, p_last)
            x2 = _rwkv_out(x2, y.reshape(b * s, d), bonus.reshape(b * s, d), g.reshape(b * s, d),
                           ln_x_g[o], ln_x_b[o], w_o[o])
        last = layer == depth - 1
        x2 = _ffn(x2, ffn_norm[layer, 1], wg[layer, 1], wu[layer, 1], wd[layer, 1], final_norm if last else None)
    return x2.reshape(b, s, d)
```

```python
import functools
import math

import jax
import jax.numpy as jnp
from jax import lax
from jax.experimental import pallas as pl
from jax.experimental.pallas import tpu as pltpu

F32 = jnp.float32
BF16 = jnp.bfloat16

NORM_EPS = 1e-6
FFN_RES_WEIGHT = 0.5
CONV_LN_EPS = 1e-5
MLA_HEADS = 8
QK_NOPE = 128
QK_ROPE = 64
V_HEAD = 128
QK_DIM = QK_NOPE + QK_ROPE
ROPE_THETA = 10000.0
RWKV_HEAD = 64
RWKV_GN_EPS = 64e-5

LANES = 128
SUBLANES = 8
MXU_TILE = 256
VMEM_LIMIT = 56 * 1024 * 1024
NEG_BIG = -1e30

SCAN_CHUNK = 64
SCAN_HEADS = 2


def _params(sem):
    return pltpu.CompilerParams(dimension_semantics=sem, vmem_limit_bytes=VMEM_LIMIT)


def _dot(a, b):
    return jnp.dot(a, b, preferred_element_type=F32)


def _dot_nt(a, b):
    return lax.dot_general(a, b, (((1,), (1,)), ((), ())), preferred_element_type=F32)


def _dot_tn(a, b):
    return lax.dot_general(a, b, (((0,), (0,)), ((), ())), preferred_element_type=F32)


def _rms(x, g, eps):
    ms = jnp.mean(x * x, axis=-1, keepdims=True)
    return x * lax.rsqrt(ms + eps) * g


def _hilo(x):
    hi = x.astype(BF16)
    lo = (x - hi.astype(F32)).astype(BF16)
    return hi, lo


def _const_spec(shape):
    nd = len(shape)
    return pl.BlockSpec(shape, lambda *_: (0,) * nd)


def _ffn_body(*refs, final, tf):
    if final:
        x_ref, g_ref, wg_ref, wu_ref, wd_ref, fg_ref, o_ref = refs
    else:
        x_ref, g_ref, wg_ref, wu_ref, wd_ref, o_ref = refs
    x = x_ref[...]
    h = _rms(x, g_ref[...], NORM_EPS).astype(BF16)
    acc = None
    for c0 in range(0, wg_ref.shape[1], tf):
        gate = _dot(h, wg_ref[:, c0:c0 + tf])
        up = _dot(h, wu_ref[:, c0:c0 + tf])
        act = (gate * jax.nn.sigmoid(gate) * up).astype(BF16)
        part = _dot(act, wd_ref[c0:c0 + tf, :])
        acc = part if acc is None else acc + part
    y = x + FFN_RES_WEIGHT * acc
    if final:
        y = _rms(y, fg_ref[...], NORM_EPS)
    o_ref[...] = y


def _resident_spec(shape, lead):
    return pl.BlockSpec((None,) * len(lead) + shape, lambda *_: lead + (0,) * len(shape), pipeline_mode=pl.Buffered(1))


def _ffn(x2, g, wg, wu, wd, which, final_g=None, *, tm=1024, tf=256):
    t, d = x2.shape
    ff = wg.shape[-1]
    tm = min(tm, t)
    assert ff % tf == 0 and t % tm == 0
    final = final_g is not None
    tok = pl.BlockSpec((tm, d), lambda i: (i, 0))
    in_specs = [tok, _const_spec((1, d)), _resident_spec((d, ff), which), _resident_spec((d, ff), which),
                _resident_spec((ff, d), which)]
    args = [x2, g.reshape(1, d), wg, wu, wd]
    if final:
        in_specs.append(_const_spec((1, d)))
        args.append(final_g.reshape(1, d))
    return pl.pallas_call(
        functools.partial(_ffn_body, final=final, tf=tf),
        grid=(t // tm,),
        in_specs=in_specs,
        out_specs=tok,
        out_shape=jax.ShapeDtypeStruct((t, d), F32),
        compiler_params=_params(("parallel",)),
        name="ffn_final" if final else "ffn",
    )(*args)


CONV_HALO = 32
CONV_ROWS = 64


def _mix_in_body(x_ref, pos_ref, g_ref, wa_ref, wgt_ref, wql_ref, wkvl_ref, wkpe_ref, wkpes_ref,
                 cw_ref, cb_ref, lng_ref, lnb_ref, qn_ref, wqn_ref, wqp_ref, wqps_ref,
                 kvn_ref, wukv_ref, invf_ref, sgn_ref,
                 conv_ref, q_ref, k_ref, v_ref, ext_scr, acc_scr, *, tm, width):
    si = pl.program_id(1)
    d_conv = ext_scr.shape[1]
    @pl.when(si == 0)
    def _():
        ext_scr[0:CONV_HALO, :] = jnp.zeros((CONV_HALO, d_conv), F32)

    @pl.when(si > 0)
    def _():
        ext_scr[0:CONV_HALO, :] = ext_scr[tm:tm + CONV_HALO, :]

    hn = _rms(x_ref[0], g_ref[...], NORM_EPS).astype(BF16)

    q_lat = _dot(hn, wql_ref[...])
    kv_lat = _dot(hn, wkvl_ref[...])
    k_pe_a = _dot(hn, wkpe_ref[...])
    k_pe_b = _dot(hn, wkpes_ref[...])

    za = _dot(hn, wa_ref[...])
    zg = _dot(hn, wgt_ref[...])
    ext_scr[CONV_HALO:CONV_HALO + tm, :] = za * jax.nn.sigmoid(zg)

    ang = pos_ref[0].astype(F32) * invf_ref[...]
    cos1 = jnp.cos(ang)
    sin1 = jnp.sin(ang) * sgn_ref[...]
    n_pe = MLA_HEADS * QK_ROPE
    cosq = jnp.concatenate([cos1] * (n_pe // LANES), axis=-1)
    sinq = jnp.concatenate([sin1] * (n_pe // LANES), axis=-1)

    qn = _rms(q_lat, qn_ref[...], NORM_EPS).astype(BF16)
    kvn = _rms(kv_lat, kvn_ref[...], NORM_EPS).astype(BF16)
    q_nope = _dot(qn, wqn_ref[...])
    q_pe = _dot(qn, wqp_ref[...]) * cosq + _dot(qn, wqps_ref[...]) * sinq
    kv = _dot(kvn, wukv_ref[...])
    k_pe = (k_pe_a * cos1 + k_pe_b * sin1)[:, 0:QK_ROPE].astype(BF16)

    for h in range(MLA_HEADS):
        q_ref[0, h, :, 0:QK_NOPE] = q_nope[:, h * QK_NOPE:(h + 1) * QK_NOPE].astype(BF16)
        q_ref[0, h, :, QK_NOPE:QK_DIM] = q_pe[:, h * QK_ROPE:(h + 1) * QK_ROPE].astype(BF16)
        c0 = h * (QK_NOPE + V_HEAD)
        k_ref[0, h, :, 0:QK_NOPE] = kv[:, c0:c0 + QK_NOPE].astype(BF16)
        k_ref[0, h, :, QK_NOPE:QK_DIM] = k_pe
        v_ref[0, h] = kv[:, c0 + QK_NOPE:c0 + QK_NOPE + V_HEAD].astype(BF16)

    base = CONV_HALO - (width - 1)

    def col_block(c, carry):
        cols = pl.ds(pl.multiple_of(c * LANES, LANES), LANES)
        w = cw_ref[:, cols]
        b = cb_ref[:, cols]
        for r0 in range(0, tm, CONV_ROWS):
            acc = jnp.broadcast_to(b, (CONV_ROWS, LANES))
            for rho in range(SUBLANES):
                taps = [j for j in range(width) if (base + j) % SUBLANES == rho]
                if not taps:
                    continue
                rows = CONV_ROWS + (SUBLANES if rho else 0)
                part = None
                for j in taps:
                    off = r0 + base + j - rho
                    term = w[j:j + 1, :] * ext_scr[off:off + rows, cols]
                    part = term if part is None else part + term
                acc = acc + part[rho:rho + CONV_ROWS, :]
            acc_scr[r0:r0 + CONV_ROWS, cols] = acc
        return carry

    lax.fori_loop(0, d_conv // LANES, col_block, 0)
    hc = acc_scr[...]
    mu = jnp.mean(hc, axis=-1, keepdims=True)
    dc = hc - mu
    var = jnp.mean(dc * dc, axis=-1, keepdims=True)
    hcn = dc * lax.rsqrt(var + CONV_LN_EPS) * lng_ref[...] + lnb_ref[...]
    conv_ref[0] = (hcn * jax.nn.sigmoid(hcn)).astype(BF16)


def _swap_halves(w, block):
    k, n = w.shape
    w = w.reshape(k, n // block, 2, block // 2)
    return w[:, :, ::-1, :].reshape(k, n)


def _mix_in(x, positions, norm_g, w_in, conv_w, conv_b, ln_g, ln_b, q_norm, w_uq, kv_norm, w_ukv, *, tm=512):
    b, s, d = x.shape
    tm = min(tm, s)
    width, d_conv = conv_w.shape
    q_lora = q_norm.shape[0]
    kv_lora = kv_norm.shape[0]
    assert width - 1 <= CONV_HALO and tm % CONV_ROWS == 0 and tm >= CONV_HALO
    o1, o2, o3 = 2 * d_conv, 2 * d_conv + q_lora, 2 * d_conv + q_lora + kv_lora
    w_a, w_gt = w_in[:, :d_conv].astype(BF16), w_in[:, d_conv:o1].astype(BF16)
    w_ql, w_kvl = w_in[:, o1:o2].astype(BF16), w_in[:, o2:o3].astype(BF16)
    w_kpe = w_in[:, o3:]
    pad = jnp.zeros((d, LANES - QK_ROPE), F32)
    w_kpes = jnp.concatenate([_swap_halves(w_kpe, QK_ROPE), pad], axis=1).astype(BF16)
    w_kpe = jnp.concatenate([w_kpe, pad], axis=1).astype(BF16)
    wq = w_uq.reshape(q_lora, MLA_HEADS, QK_DIM)
    w_qn = wq[:, :, :QK_NOPE].reshape(q_lora, MLA_HEADS * QK_NOPE).astype(BF16)
    w_qp = wq[:, :, QK_NOPE:].reshape(q_lora, MLA_HEADS * QK_ROPE)
    w_qps = _swap_halves(w_qp, QK_ROPE).astype(BF16)
    w_qp = w_qp.astype(BF16)
    inv_freq = 1.0 / (ROPE_THETA ** (jnp.arange(0, QK_ROPE, 2, dtype=F32) / QK_ROPE))
    invf = jnp.tile(inv_freq, 2 * LANES // QK_ROPE).reshape(1, LANES)
    half = QK_ROPE // 2
    sgn = jnp.tile(jnp.concatenate([-jnp.ones((half,), F32), jnp.ones((half,), F32)]), LANES // QK_ROPE).reshape(1, LANES)

    row = lambda v: v.reshape(1, -1)
    consts = [row(norm_g), w_a, w_gt, w_ql, w_kvl, w_kpe, w_kpes, conv_w, row(conv_b), row(ln_g), row(ln_b),
              row(q_norm), w_qn, w_qp, w_qps, row(kv_norm), w_ukv.astype(BF16), invf, sgn]
    in_specs = [pl.BlockSpec((1, tm, d), lambda bi, si: (bi, si, 0)),
                pl.BlockSpec((1, tm, 1), lambda bi, si: (bi, si, 0))]
    in_specs += [_const_spec(c.shape) for c in consts]
    hb = lambda bi, si: (bi, 0, si, 0)
    return pl.pallas_call(
        functools.partial(_mix_in_body, tm=tm, width=width),
        grid=(b, s // tm),
        in_specs=in_specs,
        out_specs=[pl.BlockSpec((1, tm, d_conv), lambda bi, si: (bi, si, 0)),
                   pl.BlockSpec((1, MLA_HEADS, tm, QK_DIM), hb),
                   pl.BlockSpec((1, MLA_HEADS, tm, QK_DIM), hb),
                   pl.BlockSpec((1, MLA_HEADS, tm, V_HEAD), hb)],
        out_shape=[jax.ShapeDtypeStruct((b, s, d_conv), BF16),
                   jax.ShapeDtypeStruct((b, MLA_HEADS, s, QK_DIM), BF16),
                   jax.ShapeDtypeStruct((b, MLA_HEADS, s, QK_DIM), BF16),
                   jax.ShapeDtypeStruct((b, MLA_HEADS, s, V_HEAD), BF16)],
        scratch_shapes=[pltpu.VMEM((tm + CONV_HALO, d_conv), F32), pltpu.VMEM((tm, d_conv), F32)],
        compiler_params=_params(("parallel", "arbitrary")),
        name="mix_in",
    )(x, positions.reshape(b, s, 1), *consts)


def _attn_body(q_ref, k_ref, v_ref, o_ref, *, tq, exp2_scale):
    seq = q_ref.shape[2]
    rows = lambda i: slice(i * tq, (i + 1) * tq)
    r = lax.broadcasted_iota(jnp.int32, (tq, tq), 0)
    c = lax.broadcasted_iota(jnp.int32, (tq, tq), 1)
    causal = r >= c

    def scores(qi, j):
        return _dot_nt(q_ref[0, 0, rows(qi), :], k_ref[0, 0, rows(j), :])

    pairs = [(qi, j) for qi in range(seq // tq) for j in range(qi + 1)]
    s_next = scores(*pairs[0])
    for idx, (qi, j) in enumerate(pairs):
        s = s_next
        if idx + 1 < len(pairs):
            s_next = scores(*pairs[idx + 1])
        if j == 0:
            m = jnp.full((tq, 1), NEG_BIG, F32)
            l = jnp.zeros((tq, 1), F32)
            acc = jnp.zeros((tq, V_HEAD), F32)
        if j == qi:
            s = jnp.where(causal, s, NEG_BIG)
        m_new = jnp.maximum(m, jnp.max(s, axis=-1, keepdims=True))
        p = jnp.exp2((s - m_new) * exp2_scale)
        alpha = jnp.exp2((m - m_new) * exp2_scale)
        l = alpha * l + jnp.sum(p, axis=-1, keepdims=True)
        acc = alpha * acc + _dot(p.astype(BF16), v_ref[0, 0, rows(j), :])
        m = m_new
        if j == qi:
            o_ref[0, rows(qi), :] = (acc / l).astype(BF16)


def _attention(q, k, v, *, tq=512):
    b, h, s, _ = q.shape
    tq = min(tq, s)
    assert s % tq == 0
    exp2_scale = (QK_DIM ** -0.5) * math.log2(math.e)
    head = lambda d: pl.BlockSpec((1, 1, s, d), lambda bi, hi: (bi, hi, 0, 0))
    return pl.pallas_call(
        functools.partial(_attn_body, tq=tq, exp2_scale=exp2_scale),
        grid=(b, h),
        in_specs=[head(QK_DIM), head(QK_DIM), head(V_HEAD)],
        out_specs=pl.BlockSpec((1, s, V_HEAD), lambda bi, hi: (bi, 0, hi)),
        out_shape=jax.ShapeDtypeStruct((b, s, h * V_HEAD), BF16),
        compiler_params=_params(("parallel", "parallel")),
        name="attention",
    )(q, k, v)


def _mix_out_body(x_ref, c_ref, a_ref, wc_ref, wa_ref, o_ref):
    o_ref[...] = x_ref[...] + _dot(c_ref[...], wc_ref[...]) + _dot(a_ref[...], wa_ref[...])


def _mix_out(x2, conv2, attn2, w_out, *, tm=1024):
    t, d = x2.shape
    tm = min(tm, t)
    dc, da = conv2.shape[1], attn2.shape[1]
    wc, wa = w_out[:dc].astype(BF16), w_out[dc:].astype(BF16)
    return pl.pallas_call(
        _mix_out_body,
        grid=(t // tm,),
        in_specs=[pl.BlockSpec((tm, d), lambda i: (i, 0)),
                  pl.BlockSpec((tm, dc), lambda i: (i, 0)),
                  pl.BlockSpec((tm, da), lambda i: (i, 0)),
                  _const_spec(wc.shape), _const_spec(wa.shape)],
        out_specs=pl.BlockSpec((tm, d), lambda i: (i, 0)),
        out_shape=jax.ShapeDtypeStruct((t, d), F32),
        compiler_params=_params(("parallel",)),
        name="mix_out",
    )(x2, conv2, attn2, wc, wa)


SHIFT_HALO = 8


def _headsum(x, ones_ref):
    w = ones_ref.shape[0]
    hi, lo = _hilo(x)
    blocks = [_dot(hi[:, c:c + w], ones_ref[...]) + _dot(lo[:, c:c + w], ones_ref[...]) for c in range(0, x.shape[1], w)]
    return jnp.concatenate(blocks, axis=1)


def _rwkv_in_body(x_ref, g_ref, mu_ref, wr_ref, wk_ref, wv_ref, w0_ref, w1_ref, w2_ref, a0_ref, a1_ref, a2_ref,
                  g1_ref, g2_ref, kk_ref, ka_ref, rk_ref, ones_ref, tri_ref,
                  ar_out, bk_out, v_out, pl_out, bonus_out, g_out, carry_scr, *, tm):
    si = pl.program_id(1)
    L = SCAN_CHUNK
    d = carry_scr.shape[1]
    h = _rms(x_ref[0], g_ref[...], NORM_EPS)

    @pl.when(si == 0)
    def _():
        carry_scr[...] = jnp.zeros_like(carry_scr)

    prev_last = carry_scr[SHIFT_HALO - 1:SHIFT_HALO, :]
    first_row = lax.broadcasted_iota(jnp.int32, (tm, d), 0) == 0
    hh = jnp.where(first_row, prev_last, pltpu.roll(h, 1, axis=0)) - h
    carry_scr[...] = h[tm - SHIFT_HALO:tm, :]
    mix = lambda i: (h + hh * mu_ref[i:i + 1, :]).astype(BF16)
    xr, xw, xk, xv, xa, xg = [mix(i) for i in range(6)]
    r = _dot(xr, wr_ref[...])
    k = _dot(xk, wk_ref[...])
    v = _dot(xv, wv_ref[...])
    z = w0_ref[...] + _dot(jnp.tanh(_dot(xw, w1_ref[...])).astype(BF16), w2_ref[...])
    lw = -math.exp(-0.5) * jax.nn.sigmoid(z)
    a = jax.nn.sigmoid(a0_ref[...] + _dot(_dot(xa, a1_ref[...]).astype(BF16), a2_ref[...]))
    g = _dot(jax.nn.sigmoid(_dot(xg, g1_ref[...])).astype(BF16), g2_ref[...])
    kkr = k * kk_ref[...]
    kk = kkr * lax.rsqrt(jnp.maximum(_headsum(kkr * kkr, ones_ref), 1e-24))
    k2 = k * (1.0 + (a - 1.0) * ka_ref[...])
    bonus_out[0] = _headsum(r * k2 * rk_ref[...], ones_ref) * v
    g_out[0] = g
    v_out[0] = v.astype(BF16)

    lw_hi, lw_lo = _hilo(lw)
    cum = _dot(tri_ref[...], lw_hi) + _dot(tri_ref[...], lw_lo)
    e_pos = jnp.exp(cum)
    e_neg = jnp.exp(-cum)
    at = (-kk * jnp.exp(cum - lw)).astype(BF16)
    rt = (r * e_pos).astype(BF16)
    bt = (kk * a * e_neg).astype(BF16)
    kt = (k2 * e_neg).astype(BF16)
    for c in range(tm // L):
        tok = slice(c * L, (c + 1) * L)
        ar_out[0, 2 * c * L:(2 * c + 1) * L, :] = at[tok]
        ar_out[0, (2 * c + 1) * L:(2 * c + 2) * L, :] = rt[tok]
        bk_out[0, 2 * c * L:(2 * c + 1) * L, :] = bt[tok]
        bk_out[0, (2 * c + 1) * L:(2 * c + 2) * L, :] = kt[tok]
        pl_out[0, c * SUBLANES:(c + 1) * SUBLANES, :] = jnp.broadcast_to(e_pos[(c + 1) * L - 1:(c + 1) * L, :], (SUBLANES, d))


def _pad_cols(w, n):
    return jnp.pad(w, ((0, 0), (0, n - w.shape[1])))


def _pad_rows(w, n):
    return jnp.pad(w, ((0, n - w.shape[0]), (0, 0)))


def _head_ones(d):
    w = min(MXU_TILE, d)
    assert d % w == 0 and w % RWKV_HEAD == 0
    idx = jnp.arange(w) // RWKV_HEAD
    return (idx[:, None] == idx[None, :]).astype(BF16)


def _rwkv_in(x, norm_g, time_mu, w_r, w_k, w_v, w0, w1, w2, a0, a1, a2, g1, g2, k_k, k_a, r_k, *, tm=256):
    b, s, d = x.shape
    tm = min(tm, s)
    L = SCAN_CHUNK
    assert tm % L == 0 and s % tm == 0
    row = lambda v: v.reshape(1, -1)
    lp = lambda n: ((n + LANES - 1) // LANES) * LANES
    idx = jnp.arange(tm)
    tri = ((idx[:, None] // L == idx[None, :] // L) & (idx[None, :] <= idx[:, None])).astype(BF16)
    consts = [row(norm_g), time_mu, w_r.astype(BF16), w_k.astype(BF16), w_v.astype(BF16),
              row(w0), _pad_cols(w1, lp(w1.shape[1])).astype(BF16), _pad_rows(w2, lp(w2.shape[0])).astype(BF16),
              row(a0), _pad_cols(a1, lp(a1.shape[1])).astype(BF16), _pad_rows(a2, lp(a2.shape[0])).astype(BF16),
              _pad_cols(g1, lp(g1.shape[1])).astype(BF16), _pad_rows(g2, lp(g2.shape[0])).astype(BF16),
              row(k_k), row(k_a), row(r_k), _head_ones(d), tri]
    tok = lambda rows: pl.BlockSpec((1, rows, d), lambda bi, si: (bi, si, 0))
    pl_rows = tm // L * SUBLANES
    return pl.pallas_call(
        functools.partial(_rwkv_in_body, tm=tm),
        grid=(b, s // tm),
        in_specs=[tok(tm)] + [_const_spec(c.shape) for c in consts],
        out_specs=[tok(2 * tm), tok(2 * tm), tok(tm), tok(pl_rows), tok(tm), tok(tm)],
        out_shape=[jax.ShapeDtypeStruct((b, 2 * s, d), BF16), jax.ShapeDtypeStruct((b, 2 * s, d), BF16),
                   jax.ShapeDtypeStruct((b, s, d), BF16), jax.ShapeDtypeStruct((b, s // L * SUBLANES, d), F32),
                   jax.ShapeDtypeStruct((b, s, d), F32), jax.ShapeDtypeStruct((b, s, d), F32)],
        scratch_shapes=[pltpu.VMEM((SHIFT_HALO, d), F32)],
        compiler_params=_params(("parallel", "arbitrary")),
        name="rwkv_in",
    )(x, *consts)


def _rwkv_scan_body(ar_ref, bk_ref, v_ref, pl_ref, y_ref, ht_scr, *, tile, heads):
    L = SCAN_CHUNK
    gk = heads * RWKV_HEAD
    gl = heads * L
    nb, groups = ht_scr.shape[0], ht_scr.shape[1]
    chains = [(bi, g) for bi in range(nb) for g in range(groups)]
    row = lax.broadcasted_iota(jnp.int32, (L, gl), 0)
    pos = lax.broadcasted_iota(jnp.int32, (L, gl), 1) % L
    strict = pos < row
    incl = pos <= row
    eye = (pos == row).astype(F32)
    blk = lambda shape: (lax.broadcasted_iota(jnp.int32, shape, 0) // L) % heads == lax.broadcasted_iota(jnp.int32, shape, 1) // L
    bd_mask = blk((gl, gk))
    bd2_mask = blk((2 * gl, gk))

    @pl.when(pl.program_id(1) == 0)
    def _():
        ht_scr[...] = jnp.zeros_like(ht_scr)

    def bd(x):
        return jnp.where(bd_mask, jnp.concatenate([x] * heads, axis=0), 0).astype(BF16)

    def chunk(c, carry):
        each = lambda f, *xs: [f(*a) for a in zip(*xs)]
        cat = lambda *xs: jnp.concatenate(xs, axis=0)
        load = lambda ref, n: [ref[bi, pl.ds(pl.multiple_of(c * n, n), n), g * gk:(g + 1) * gk] for bi, g in chains]
        lhs = load(ar_ref, 2 * L)
        bk = load(bk_ref, 2 * L)
        v = load(v_ref, L)
        p_last = [x[0:1, :] for x in load(pl_ref, SUBLANES)]
        ht = [ht_scr[bi, g] for bi, g in chains]
        abk = each(lambda l_, x: _dot_nt(l_, jnp.where(bd2_mask, cat(*([x[:L]] * heads + [x[L:]] * heads)), 0)), lhs, bk)
        xr = each(lambda l_, h_: _dot_nt(l_, h_.astype(BF16)), lhs, ht)
        a_ab = [jnp.where(strict, x[:L, :gl], 0.0) for x in abk]
        a_rb = [jnp.where(incl, x[L:, :gl], 0.0) for x in abk]
        av = each(lambda x, v_: _dot(cat(jnp.where(strict, x[:L, gl:], 0.0), jnp.where(incl, x[L:, gl:], 0.0)).astype(BF16), bd(v_)),
                  abk, v)
        t = [eye + x for x in a_ab]
        p = each(lambda x: _dot(x.astype(BF16), bd(x)), a_ab)
        rounds = int(math.log2(L)) - 1
        for i in range(rounds - 1):
            res = each(lambda p_, t_: _dot(p_.astype(BF16), jnp.concatenate([bd(p_), bd(t_)], axis=1)), p, t)
            p = [x[:, :gl] for x in res]
            t = each(lambda t_, x: t_ + x[:, gl:], t, res)
        t = each(lambda t_, p_: t_ + _dot(p_.astype(BF16), bd(t_)), t, p)
        u = each(lambda t_, x, a_: _dot(t_.astype(BF16), bd(x[:L] + a_[:L])), t, xr, av)
        yu = each(lambda a_, u_: _dot(a_.astype(BF16), bd(u_)), a_rb, u)
        upd = each(lambda u_, v_, x: _dot_tn(cat(u_.astype(BF16), v_), x), u, v, bk)
        for i, (bi, g) in enumerate(chains):
            y_ref[bi, pl.ds(pl.multiple_of(c * L, L), L), g * gk:(g + 1) * gk] = xr[i][L:] + av[i][L:] + yu[i]
            ht_scr[bi, g] = (ht[i] + jnp.where(bd_mask, upd[i], 0.0)) * p_last[i]
        return carry

    lax.fori_loop(0, tile // L, chunk, 0)


def _rwkv_scan(ar, bk, v, p_last, *, nb=4, tile=256):
    b, s, d = v.shape
    gk = SCAN_HEADS * RWKV_HEAD
    L = SCAN_CHUNK
    tile = min(tile, s)
    nb = min(nb, b)
    assert L == RWKV_HEAD and tile % L == 0 and s % tile == 0 and d % gk == 0 and b % nb == 0
    spec = lambda rows: pl.BlockSpec((nb, rows, d), lambda bi, si: (bi, si, 0))
    return pl.pallas_call(
        functools.partial(_rwkv_scan_body, tile=tile, heads=SCAN_HEADS),
        grid=(b // nb, s // tile),
        in_specs=[spec(2 * tile), spec(2 * tile), spec(tile), spec(tile // L * SUBLANES)],
        out_specs=spec(tile),
        out_shape=jax.ShapeDtypeStruct((b, s, d), F32),
        scratch_shapes=[pltpu.VMEM((nb, d // gk, gk, gk), F32)],
        compiler_params=_params(("parallel", "arbitrary")),
        name="rwkv_scan",
    )(ar, bk, v, p_last)


def _rwkv_out_body(x_ref, y_ref, bonus_ref, g_ref, lng_ref, lnb_ref, wo_ref, ones_ref, o_ref):
    y = y_ref[...]
    inv_n = 1.0 / RWKV_HEAD
    mu = _headsum(y, ones_ref) * inv_n
    dy = y - mu
    var = _headsum(dy * dy, ones_ref) * inv_n
    yn = dy * lax.rsqrt(var + RWKV_GN_EPS) * lng_ref[...] + lnb_ref[...]
    o_ref[...] = x_ref[...] + _dot(((yn + bonus_ref[...]) * g_ref[...]).astype(BF16), wo_ref[...])


def _rwkv_out(x2, y2, bonus2, g2, ln_g, ln_b, w_o, *, tm=512):
    t, d = x2.shape
    tm = min(tm, t)
    tok = pl.BlockSpec((tm, d), lambda i: (i, 0))
    consts = [ln_g.reshape(1, d), ln_b.reshape(1, d), w_o.astype(BF16), _head_ones(d)]
    return pl.pallas_call(
        _rwkv_out_body,
        grid=(t // tm,),
        in_specs=[tok] * 4 + [_const_spec(c.shape) for c in consts],
        out_specs=tok,
        out_shape=jax.ShapeDtypeStruct((t, d), F32),
        compiler_params=_params(("parallel",)),
        name="rwkv_out",
    )(x2, y2, bonus2, g2, *consts)


def kernel(x, positions, ffn_norm, ffn_w_gate, ffn_w_up, ffn_w_down, mix_norm_even, w_in, conv_w, conv_b, conv_ln_g, conv_ln_b, q_norm, w_uq, kv_norm, w_ukv, w_out, mix_norm_odd, time_mu, w_r, w_k, w_v, w_o, w0, w1, w2, a0, a1, a2, g1, g2, k_k, k_a, r_k, ln_x_g, ln_x_b, final_norm):
    b, s, d = x.shape
    depth = ffn_norm.shape[0]
    x2 = x.reshape(b * s, d)
    wg, wu, wd = ffn_w_gate.astype(BF16), ffn_w_up.astype(BF16), ffn_w_down.astype(BF16)
    for layer in range(depth):
        x2 = _ffn(x2, ffn_norm[layer, 0], wg, wu, wd, (layer, 0))
        if layer % 2 == 0:
            e = layer // 2
            conv, q, k, v = _mix_in(x2.reshape(b, s, d), positions, mix_norm_even[e], w_in[e], conv_w[e], conv_b[e],
                                    conv_ln_g[e], conv_ln_b[e], q_norm[e], w_uq[e], kv_norm[e], w_ukv[e])
            attn = _attention(q, k, v)
            x2 = _mix_out(x2, conv.reshape(b * s, -1), attn.reshape(b * s, -1), w_out[e])
        else:
            o = layer // 2
            ar, bk, v, p_last, bonus, g = _rwkv_in(
                x2.reshape(b, s, d), mix_norm_odd[o], time_mu[o], w_r[o], w_k[o], w_v[o], w0[o], w1[o], w2[o],
                a0[o], a1[o], a2[o], g1[o], g2[o], k_k[o], k_a[o], r_k[o].reshape(-1))
            y = _rwkv_scan(ar, bk, v, p_last)
            x2 = _rwkv_out(x2, y.reshape(b * s, d), bonus.reshape(b * s, d), g.reshape(b * s, d),
                           ln_x_g[o], ln_x_b[o], w_o[o])
        last = layer == depth - 1
        x2 = _ffn(x2, ffn_norm[layer, 1], wg, wu, wd, (layer, 1), final_norm if last else None)
    return x2.reshape(b, s, d)
```

```python
import functools
import math

import jax
import jax.numpy as jnp
from jax import lax
from jax.experimental import pallas as pl
from jax.experimental.pallas import tpu as pltpu

F32 = jnp.float32
BF16 = jnp.bfloat16

NORM_EPS = 1e-6
FFN_RES_WEIGHT = 0.5
CONV_LN_EPS = 1e-5
MLA_HEADS = 8
QK_NOPE = 128
QK_ROPE = 64
V_HEAD = 128
QK_DIM = QK_NOPE + QK_ROPE
ROPE_THETA = 10000.0
RWKV_HEAD = 64
RWKV_GN_EPS = 64e-5

LANES = 128
SUBLANES = 8
MXU_TILE = 256
VMEM_LIMIT = 56 * 1024 * 1024
NEG_BIG = -1e30

SCAN_CHUNK = 64
SCAN_HEADS = 2


def _params(sem):
    return pltpu.CompilerParams(dimension_semantics=sem, vmem_limit_bytes=VMEM_LIMIT)


def _dot(a, b):
    return jnp.dot(a, b, preferred_element_type=F32)


def _dot_nt(a, b):
    return lax.dot_general(a, b, (((1,), (1,)), ((), ())), preferred_element_type=F32)


def _dot_tn(a, b):
    return lax.dot_general(a, b, (((0,), (0,)), ((), ())), preferred_element_type=F32)


def _rms(x, g, eps):
    ms = jnp.mean(x * x, axis=-1, keepdims=True)
    return x * lax.rsqrt(ms + eps) * g


def _hilo(x):
    hi = x.astype(BF16)
    lo = (x - hi.astype(F32)).astype(BF16)
    return hi, lo


def _const_spec(shape):
    nd = len(shape)
    return pl.BlockSpec(shape, lambda *_: (0,) * nd)


def _ffn_body(*refs, final, tf, rwkv_pre):
    refs = list(refs)
    x_ref = refs.pop(0)
    if rwkv_pre:
        pre_refs, refs = refs[:7], refs[7:]
    g_ref, wg_ref, wu_ref, wd_ref = refs[:4]
    fg_ref = refs[4] if final else None
    o_ref = refs[-1]
    x = x_ref[...]
    if rwkv_pre:
        x = x + _rwkv_mixer_out(*pre_refs)
    h = _rms(x, g_ref[...], NORM_EPS).astype(BF16)
    acc = None
    for c0 in range(0, wg_ref.shape[1], tf):
        gate = _dot(h, wg_ref[:, c0:c0 + tf])
        up = _dot(h, wu_ref[:, c0:c0 + tf])
        act = (gate * jax.nn.sigmoid(gate) * up).astype(BF16)
        part = _dot(act, wd_ref[c0:c0 + tf, :])
        acc = part if acc is None else acc + part
    y = x + FFN_RES_WEIGHT * acc
    if final:
        y = _rms(y, fg_ref[...], NORM_EPS)
    o_ref[...] = y


def _resident_spec(shape, lead):
    return pl.BlockSpec((None,) * len(lead) + shape, lambda *_: lead + (0,) * len(shape), pipeline_mode=pl.Buffered(1))


def _ffn(x2, g, wg, wu, wd, which, final_g=None, rwkv_pre=None, *, tm=1024, tf=256):
    t, d = x2.shape
    ff = wg.shape[-1]
    if rwkv_pre is not None:
        tm //= 2
    tm = min(tm, t)
    assert ff % tf == 0 and t % tm == 0
    final = final_g is not None
    tok = pl.BlockSpec((tm, d), lambda i: (i, 0))
    in_specs, args = [tok], [x2]
    if rwkv_pre is not None:
        y2, bonus2, gate2, ln_g, ln_b, w_o = rwkv_pre
        consts = [ln_g.reshape(1, d), ln_b.reshape(1, d), w_o.astype(BF16), _head_ones(d)]
        in_specs += [tok] * 3 + [_const_spec(c.shape) for c in consts]
        args += [y2, bonus2, gate2] + consts
    in_specs += [_const_spec((1, d)), _resident_spec((d, ff), which), _resident_spec((d, ff), which),
                 _resident_spec((ff, d), which)]
    args += [g.reshape(1, d), wg, wu, wd]
    if final:
        in_specs.append(_const_spec((1, d)))
        args.append(final_g.reshape(1, d))
    return pl.pallas_call(
        functools.partial(_ffn_body, final=final, tf=tf, rwkv_pre=rwkv_pre is not None),
        grid=(t // tm,),
        in_specs=in_specs,
        out_specs=tok,
        out_shape=jax.ShapeDtypeStruct((t, d), F32),
        compiler_params=_params(("parallel",)),
        name=("rwkv_out_" if rwkv_pre is not None else "") + ("ffn_final" if final else "ffn"),
    )(*args)


CONV_HALO = 32
CONV_ROWS = 64


def _mix_in_body(x_ref, pos_ref, g_ref, wa_ref, wgt_ref, wql_ref, wkvl_ref, wkpe_ref, wkpes_ref,
                 cw_ref, cb_ref, lng_ref, lnb_ref, qn_ref, wqn_ref, wqp_ref, wqps_ref,
                 kvn_ref, wukv_ref, invf_ref, sgn_ref,
                 conv_ref, q_ref, k_ref, v_ref, ext_scr, acc_scr, *, tm, width):
    si = pl.program_id(1)
    d_conv = ext_scr.shape[1]
    @pl.when(si == 0)
    def _():
        ext_scr[0:CONV_HALO, :] = jnp.zeros((CONV_HALO, d_conv), F32)

    @pl.when(si > 0)
    def _():
        ext_scr[0:CONV_HALO, :] = ext_scr[tm:tm + CONV_HALO, :]

    hn = _rms(x_ref[0], g_ref[...], NORM_EPS).astype(BF16)

    q_lat = _dot(hn, wql_ref[...])
    kv_lat = _dot(hn, wkvl_ref[...])
    k_pe_a = _dot(hn, wkpe_ref[...])
    k_pe_b = _dot(hn, wkpes_ref[...])

    za = _dot(hn, wa_ref[...])
    zg = _dot(hn, wgt_ref[...])
    ext_scr[CONV_HALO:CONV_HALO + tm, :] = za * jax.nn.sigmoid(zg)

    ang = pos_ref[0].astype(F32) * invf_ref[...]
    cos1 = jnp.cos(ang)
    sin1 = jnp.sin(ang) * sgn_ref[...]
    n_pe = MLA_HEADS * QK_ROPE
    cosq = jnp.concatenate([cos1] * (n_pe // LANES), axis=-1)
    sinq = jnp.concatenate([sin1] * (n_pe // LANES), axis=-1)

    qn = _rms(q_lat, qn_ref[...], NORM_EPS).astype(BF16)
    kvn = _rms(kv_lat, kvn_ref[...], NORM_EPS).astype(BF16)
    q_nope = _dot(qn, wqn_ref[...])
    q_pe = _dot(qn, wqp_ref[...]) * cosq + _dot(qn, wqps_ref[...]) * sinq
    kv = _dot(kvn, wukv_ref[...])
    k_pe = (k_pe_a * cos1 + k_pe_b * sin1)[:, 0:QK_ROPE].astype(BF16)

    for h in range(MLA_HEADS):
        q_ref[0, h, :, 0:QK_NOPE] = q_nope[:, h * QK_NOPE:(h + 1) * QK_NOPE].astype(BF16)
        q_ref[0, h, :, QK_NOPE:QK_DIM] = q_pe[:, h * QK_ROPE:(h + 1) * QK_ROPE].astype(BF16)
        c0 = h * (QK_NOPE + V_HEAD)
        k_ref[0, h, :, 0:QK_NOPE] = kv[:, c0:c0 + QK_NOPE].astype(BF16)
        k_ref[0, h, :, QK_NOPE:QK_DIM] = k_pe
        v_ref[0, h] = kv[:, c0 + QK_NOPE:c0 + QK_NOPE + V_HEAD].astype(BF16)

    base = CONV_HALO - (width - 1)

    def col_block(c, carry):
        cols = pl.ds(pl.multiple_of(c * LANES, LANES), LANES)
        w = cw_ref[:, cols]
        b = cb_ref[:, cols]
        for r0 in range(0, tm, CONV_ROWS):
            acc = jnp.broadcast_to(b, (CONV_ROWS, LANES))
            for rho in range(SUBLANES):
                taps = [j for j in range(width) if (base + j) % SUBLANES == rho]
                if not taps:
                    continue
                rows = CONV_ROWS + (SUBLANES if rho else 0)
                part = None
                for j in taps:
                    off = r0 + base + j - rho
                    term = w[j:j + 1, :] * ext_scr[off:off + rows, cols]
                    part = term if part is None else part + term
                acc = acc + part[rho:rho + CONV_ROWS, :]
            acc_scr[r0:r0 + CONV_ROWS, cols] = acc
        return carry

    lax.fori_loop(0, d_conv // LANES, col_block, 0)
    hc = acc_scr[...]
    mu = jnp.mean(hc, axis=-1, keepdims=True)
    dc = hc - mu
    var = jnp.mean(dc * dc, axis=-1, keepdims=True)
    hcn = dc * lax.rsqrt(var + CONV_LN_EPS) * lng_ref[...] + lnb_ref[...]
    conv_ref[0] = (hcn * jax.nn.sigmoid(hcn)).astype(BF16)


def _swap_halves(w, block):
    k, n = w.shape
    w = w.reshape(k, n // block, 2, block // 2)
    return w[:, :, ::-1, :].reshape(k, n)


def _mix_in(x, positions, norm_g, w_in, conv_w, conv_b, ln_g, ln_b, q_norm, w_uq, kv_norm, w_ukv, *, tm=512):
    b, s, d = x.shape
    tm = min(tm, s)
    width, d_conv = conv_w.shape
    q_lora = q_norm.shape[0]
    kv_lora = kv_norm.shape[0]
    assert width - 1 <= CONV_HALO and tm % CONV_ROWS == 0 and tm >= CONV_HALO
    o1, o2, o3 = 2 * d_conv, 2 * d_conv + q_lora, 2 * d_conv + q_lora + kv_lora
    w_a, w_gt = w_in[:, :d_conv].astype(BF16), w_in[:, d_conv:o1].astype(BF16)
    w_ql, w_kvl = w_in[:, o1:o2].astype(BF16), w_in[:, o2:o3].astype(BF16)
    w_kpe = w_in[:, o3:]
    pad = jnp.zeros((d, LANES - QK_ROPE), F32)
    w_kpes = jnp.concatenate([_swap_halves(w_kpe, QK_ROPE), pad], axis=1).astype(BF16)
    w_kpe = jnp.concatenate([w_kpe, pad], axis=1).astype(BF16)
    wq = w_uq.reshape(q_lora, MLA_HEADS, QK_DIM)
    w_qn = wq[:, :, :QK_NOPE].reshape(q_lora, MLA_HEADS * QK_NOPE).astype(BF16)
    w_qp = wq[:, :, QK_NOPE:].reshape(q_lora, MLA_HEADS * QK_ROPE)
    w_qps = _swap_halves(w_qp, QK_ROPE).astype(BF16)
    w_qp = w_qp.astype(BF16)
    inv_freq = 1.0 / (ROPE_THETA ** (jnp.arange(0, QK_ROPE, 2, dtype=F32) / QK_ROPE))
    invf = jnp.tile(inv_freq, 2 * LANES // QK_ROPE).reshape(1, LANES)
    half = QK_ROPE // 2
    sgn = jnp.tile(jnp.concatenate([-jnp.ones((half,), F32), jnp.ones((half,), F32)]), LANES // QK_ROPE).reshape(1, LANES)

    row = lambda v: v.reshape(1, -1)
    consts = [row(norm_g), w_a, w_gt, w_ql, w_kvl, w_kpe, w_kpes, conv_w, row(conv_b), row(ln_g), row(ln_b),
              row(q_norm), w_qn, w_qp, w_qps, row(kv_norm), w_ukv.astype(BF16), invf, sgn]
    in_specs = [pl.BlockSpec((1, tm, d), lambda bi, si: (bi, si, 0)),
                pl.BlockSpec((1, tm, 1), lambda bi, si: (bi, si, 0))]
    in_specs += [_const_spec(c.shape) for c in consts]
    hb = lambda bi, si: (bi, 0, si, 0)
    return pl.pallas_call(
        functools.partial(_mix_in_body, tm=tm, width=width),
        grid=(b, s // tm),
        in_specs=in_specs,
        out_specs=[pl.BlockSpec((1, tm, d_conv), lambda bi, si: (bi, si, 0)),
                   pl.BlockSpec((1, MLA_HEADS, tm, QK_DIM), hb),
                   pl.BlockSpec((1, MLA_HEADS, tm, QK_DIM), hb),
                   pl.BlockSpec((1, MLA_HEADS, tm, V_HEAD), hb)],
        out_shape=[jax.ShapeDtypeStruct((b, s, d_conv), BF16),
                   jax.ShapeDtypeStruct((b, MLA_HEADS, s, QK_DIM), BF16),
                   jax.ShapeDtypeStruct((b, MLA_HEADS, s, QK_DIM), BF16),
                   jax.ShapeDtypeStruct((b, MLA_HEADS, s, V_HEAD), BF16)],
        scratch_shapes=[pltpu.VMEM((tm + CONV_HALO, d_conv), F32), pltpu.VMEM((tm, d_conv), F32)],
        compiler_params=_params(("parallel", "arbitrary")),
        name="mix_in",
    )(x, positions.reshape(b, s, 1), *consts)


def _attn_body(q_ref, k_ref, v_ref, o_ref, *, tq, exp2_scale):
    seq = q_ref.shape[2]
    rows = lambda i: slice(i * tq, (i + 1) * tq)
    r = lax.broadcasted_iota(jnp.int32, (tq, tq), 0)
    c = lax.broadcasted_iota(jnp.int32, (tq, tq), 1)
    causal = r >= c

    def scores(qi, j):
        return _dot_nt(q_ref[0, 0, rows(qi), :], k_ref[0, 0, rows(j), :])

    pairs = [(qi, j) for qi in range(seq // tq) for j in range(qi + 1)]
    s_next = scores(*pairs[0])
    for idx, (qi, j) in enumerate(pairs):
        s = s_next
        if idx + 1 < len(pairs):
            s_next = scores(*pairs[idx + 1])
        if j == 0:
            m = jnp.full((tq, 1), NEG_BIG, F32)
            l = jnp.zeros((tq, 1), F32)
            acc = jnp.zeros((tq, V_HEAD), F32)
        if j == qi:
            s = jnp.where(causal, s, NEG_BIG)
        m_new = jnp.maximum(m, jnp.max(s, axis=-1, keepdims=True))
        p = jnp.exp2((s - m_new) * exp2_scale)
        alpha = jnp.exp2((m - m_new) * exp2_scale)
        l = alpha * l + jnp.sum(p, axis=-1, keepdims=True)
        acc = alpha * acc + _dot(p.astype(BF16), v_ref[0, 0, rows(j), :])
        m = m_new
        if j == qi:
            o_ref[0, rows(qi), :] = (acc / l).astype(BF16)


def _attention(q, k, v, *, tq=512):
    b, h, s, _ = q.shape
    tq = min(tq, s)
    assert s % tq == 0
    exp2_scale = (QK_DIM ** -0.5) * math.log2(math.e)
    head = lambda d: pl.BlockSpec((1, 1, s, d), lambda bi, hi: (bi, hi, 0, 0))
    return pl.pallas_call(
        functools.partial(_attn_body, tq=tq, exp2_scale=exp2_scale),
        grid=(b, h),
        in_specs=[head(QK_DIM), head(QK_DIM), head(V_HEAD)],
        out_specs=pl.BlockSpec((1, s, V_HEAD), lambda bi, hi: (bi, 0, hi)),
        out_shape=jax.ShapeDtypeStruct((b, s, h * V_HEAD), BF16),
        compiler_params=_params(("parallel", "parallel")),
        name="attention",
    )(q, k, v)


def _mix_out_body(x_ref, c_ref, a_ref, wc_ref, wa_ref, o_ref):
    o_ref[...] = x_ref[...] + _dot(c_ref[...], wc_ref[...]) + _dot(a_ref[...], wa_ref[...])


def _mix_out(x2, conv2, attn2, w_out, *, tm=1024):
    t, d = x2.shape
    tm = min(tm, t)
    dc, da = conv2.shape[1], attn2.shape[1]
    wc, wa = w_out[:dc].astype(BF16), w_out[dc:].astype(BF16)
    return pl.pallas_call(
        _mix_out_body,
        grid=(t // tm,),
        in_specs=[pl.BlockSpec((tm, d), lambda i: (i, 0)),
                  pl.BlockSpec((tm, dc), lambda i: (i, 0)),
                  pl.BlockSpec((tm, da), lambda i: (i, 0)),
                  _const_spec(wc.shape), _const_spec(wa.shape)],
        out_specs=pl.BlockSpec((tm, d), lambda i: (i, 0)),
        out_shape=jax.ShapeDtypeStruct((t, d), F32),
        compiler_params=_params(("parallel",)),
        name="mix_out",
    )(x2, conv2, attn2, wc, wa)


SHIFT_HALO = 8


def _headsum(x, ones_ref):
    w = ones_ref.shape[0]
    hi, lo = _hilo(x)
    blocks = [_dot(hi[:, c:c + w], ones_ref[...]) + _dot(lo[:, c:c + w], ones_ref[...]) for c in range(0, x.shape[1], w)]
    return jnp.concatenate(blocks, axis=1)


def _rwkv_in_body(x_ref, g_ref, mu_ref, wr_ref, wk_ref, wv_ref, w0_ref, w1_ref, w2_ref, a0_ref, a1_ref, a2_ref,
                  g1_ref, g2_ref, kk_ref, ka_ref, rk_ref, ones_ref, tri_ref,
                  ar_out, bk_out, v_out, pl_out, bonus_out, g_out, carry_scr, *, tm):
    si = pl.program_id(1)
    L = SCAN_CHUNK
    d = carry_scr.shape[1]
    h = _rms(x_ref[0], g_ref[...], NORM_EPS)

    @pl.when(si == 0)
    def _():
        carry_scr[...] = jnp.zeros_like(carry_scr)

    prev_last = carry_scr[SHIFT_HALO - 1:SHIFT_HALO, :]
    first_row = lax.broadcasted_iota(jnp.int32, (tm, d), 0) == 0
    hh = jnp.where(first_row, prev_last, pltpu.roll(h, 1, axis=0)) - h
    carry_scr[...] = h[tm - SHIFT_HALO:tm, :]
    mix = lambda i: (h + hh * mu_ref[i:i + 1, :]).astype(BF16)
    xr, xw, xk, xv, xa, xg = [mix(i) for i in range(6)]
    lora_w = _dot(xw, w1_ref[...])
    lora_a = _dot(xa, a1_ref[...])
    lora_g = _dot(xg, g1_ref[...])
    r = _dot(xr, wr_ref[...])
    k = _dot(xk, wk_ref[...])
    v = _dot(xv, wv_ref[...])
    z = w0_ref[...] + _dot(jnp.tanh(lora_w).astype(BF16), w2_ref[...])
    lw = -math.exp(-0.5) * jax.nn.sigmoid(z)
    a = jax.nn.sigmoid(a0_ref[...] + _dot(lora_a.astype(BF16), a2_ref[...]))
    g = _dot(jax.nn.sigmoid(lora_g).astype(BF16), g2_ref[...])
    kkr = k * kk_ref[...]
    kk = kkr * lax.rsqrt(jnp.maximum(_headsum(kkr * kkr, ones_ref), 1e-24))
    k2 = k * (1.0 + (a - 1.0) * ka_ref[...])
    bonus_out[0] = _headsum(r * k2 * rk_ref[...], ones_ref) * v
    g_out[0] = g
    v_out[0] = v.astype(BF16)

    lw_hi, lw_lo = _hilo(lw)
    cum = _dot(tri_ref[...], lw_hi) + _dot(tri_ref[...], lw_lo)
    e_pos = jnp.exp(cum)
    e_neg = jnp.exp(-cum)
    at = (-kk * jnp.exp(cum - lw)).astype(BF16)
    rt = (r * e_pos).astype(BF16)
    bt = (kk * a * e_neg).astype(BF16)
    kt = (k2 * e_neg).astype(BF16)
    for c in range(tm // L):
        tok = slice(c * L, (c + 1) * L)
        ar_out[0, 2 * c * L:(2 * c + 1) * L, :] = at[tok]
        ar_out[0, (2 * c + 1) * L:(2 * c + 2) * L, :] = rt[tok]
        bk_out[0, 2 * c * L:(2 * c + 1) * L, :] = bt[tok]
        bk_out[0, (2 * c + 1) * L:(2 * c + 2) * L, :] = kt[tok]
        pl_out[0, c * SUBLANES:(c + 1) * SUBLANES, :] = jnp.broadcast_to(e_pos[(c + 1) * L - 1:(c + 1) * L, :], (SUBLANES, d))


def _pad_cols(w, n):
    return jnp.pad(w, ((0, 0), (0, n - w.shape[1])))


def _pad_rows(w, n):
    return jnp.pad(w, ((0, n - w.shape[0]), (0, 0)))


def _head_ones(d):
    w = min(MXU_TILE, d)
    assert d % w == 0 and w % RWKV_HEAD == 0
    idx = jnp.arange(w) // RWKV_HEAD
    return (idx[:, None] == idx[None, :]).astype(BF16)


def _rwkv_in(x, norm_g, time_mu, w_r, w_k, w_v, w0, w1, w2, a0, a1, a2, g1, g2, k_k, k_a, r_k, *, tm=256):
    b, s, d = x.shape
    tm = min(tm, s)
    L = SCAN_CHUNK
    assert tm % L == 0 and s % tm == 0
    row = lambda v: v.reshape(1, -1)
    lp = lambda n: ((n + LANES - 1) // LANES) * LANES
    idx = jnp.arange(tm)
    tri = ((idx[:, None] // L == idx[None, :] // L) & (idx[None, :] <= idx[:, None])).astype(BF16)
    consts = [row(norm_g), time_mu, w_r.astype(BF16), w_k.astype(BF16), w_v.astype(BF16),
              row(w0), _pad_cols(w1, lp(w1.shape[1])).astype(BF16), _pad_rows(w2, lp(w2.shape[0])).astype(BF16),
              row(a0), _pad_cols(a1, lp(a1.shape[1])).astype(BF16), _pad_rows(a2, lp(a2.shape[0])).astype(BF16),
              _pad_cols(g1, lp(g1.shape[1])).astype(BF16), _pad_rows(g2, lp(g2.shape[0])).astype(BF16),
              row(k_k), row(k_a), row(r_k), _head_ones(d), tri]
    tok = lambda rows: pl.BlockSpec((1, rows, d), lambda bi, si: (bi, si, 0))
    pl_rows = tm // L * SUBLANES
    return pl.pallas_call(
        functools.partial(_rwkv_in_body, tm=tm),
        grid=(b, s // tm),
        in_specs=[tok(tm)] + [_const_spec(c.shape) for c in consts],
        out_specs=[tok(2 * tm), tok(2 * tm), tok(tm), tok(pl_rows), tok(tm), tok(tm)],
        out_shape=[jax.ShapeDtypeStruct((b, 2 * s, d), BF16), jax.ShapeDtypeStruct((b, 2 * s, d), BF16),
                   jax.ShapeDtypeStruct((b, s, d), BF16), jax.ShapeDtypeStruct((b, s // L * SUBLANES, d), F32),
                   jax.ShapeDtypeStruct((b, s, d), F32), jax.ShapeDtypeStruct((b, s, d), F32)],
        scratch_shapes=[pltpu.VMEM((SHIFT_HALO, d), F32)],
        compiler_params=_params(("parallel", "arbitrary")),
        name="rwkv_in",
    )(x, *consts)


def _rwkv_scan_body(ar_ref, bk_ref, v_ref, pl_ref, y_ref, ht_scr, *, tile, heads):
    L = SCAN_CHUNK
    gk = heads * RWKV_HEAD
    gl = heads * L
    nb, groups = ht_scr.shape[0], ht_scr.shape[1]
    chains = [(bi, g) for bi in range(nb) for g in range(groups)]
    row = lax.broadcasted_iota(jnp.int32, (L, gl), 0)
    pos = lax.broadcasted_iota(jnp.int32, (L, gl), 1) % L
    strict = pos < row
    incl = pos <= row
    eye = (pos == row).astype(F32)
    blk = lambda shape: (lax.broadcasted_iota(jnp.int32, shape, 0) // L) % heads == lax.broadcasted_iota(jnp.int32, shape, 1) // L
    bd_mask = blk((gl, gk))
    bd2_mask = blk((2 * gl, gk))

    @pl.when(pl.program_id(1) == 0)
    def _():
        ht_scr[...] = jnp.zeros_like(ht_scr)

    def bd(x):
        return jnp.where(bd_mask, jnp.concatenate([x] * heads, axis=0), 0).astype(BF16)

    def chunk(c, carry):
        each = lambda f, *xs: [f(*a) for a in zip(*xs)]
        cat = lambda *xs: jnp.concatenate(xs, axis=0)
        load = lambda ref, n: [ref[bi, pl.ds(pl.multiple_of(c * n, n), n), g * gk:(g + 1) * gk] for bi, g in chains]
        lhs = load(ar_ref, 2 * L)
        bk = load(bk_ref, 2 * L)
        v = load(v_ref, L)
        p_last = [x[0:1, :] for x in load(pl_ref, SUBLANES)]
        ht = [ht_scr[bi, g] for bi, g in chains]
        abk = each(lambda l_, x: _dot_nt(l_, jnp.where(bd2_mask, cat(*([x[:L]] * heads + [x[L:]] * heads)), 0)), lhs, bk)
        xr = each(lambda l_, h_: _dot_nt(l_, h_.astype(BF16)), lhs, ht)
        a_ab = [jnp.where(strict, x[:L, :gl], 0.0) for x in abk]
        a_rb = [jnp.where(incl, x[L:, :gl], 0.0) for x in abk]
        av = each(lambda x, v_: _dot(cat(jnp.where(strict, x[:L, gl:], 0.0), jnp.where(incl, x[L:, gl:], 0.0)).astype(BF16), bd(v_)),
                  abk, v)
        t = [eye + x for x in a_ab]
        p = each(lambda x: _dot(x.astype(BF16), bd(x)), a_ab)
        rounds = int(math.log2(L)) - 1
        for i in range(rounds - 1):
            res = each(lambda p_, t_: _dot(p_.astype(BF16), jnp.concatenate([bd(p_), bd(t_)], axis=1)), p, t)
            p = [x[:, :gl] for x in res]
            t = each(lambda t_, x: t_ + x[:, gl:], t, res)
        t = each(lambda t_, p_: t_ + _dot(p_.astype(BF16), bd(t_)), t, p)
        u = each(lambda t_, x, a_: _dot(t_.astype(BF16), bd(x[:L] + a_[:L])), t, xr, av)
        yu = each(lambda a_, u_: _dot(a_.astype(BF16), bd(u_)), a_rb, u)
        upd = each(lambda u_, v_, x: _dot_tn(cat(u_.astype(BF16), v_), x), u, v, bk)
        for i, (bi, g) in enumerate(chains):
            y_ref[bi, pl.ds(pl.multiple_of(c * L, L), L), g * gk:(g + 1) * gk] = xr[i][L:] + av[i][L:] + yu[i]
            ht_scr[bi, g] = (ht[i] + jnp.where(bd_mask, upd[i], 0.0)) * p_last[i]
        return carry

    lax.fori_loop(0, tile // L, chunk, 0)


def _rwkv_scan(ar, bk, v, p_last, *, nb=4, tile=256):
    b, s, d = v.shape
    gk = SCAN_HEADS * RWKV_HEAD
    L = SCAN_CHUNK
    tile = min(tile, s)
    nb = min(nb, b)
    assert L == RWKV_HEAD and tile % L == 0 and s % tile == 0 and d % gk == 0 and b % nb == 0
    spec = lambda rows: pl.BlockSpec((nb, rows, d), lambda bi, si: (bi, si, 0))
    return pl.pallas_call(
        functools.partial(_rwkv_scan_body, tile=tile, heads=SCAN_HEADS),
        grid=(b // nb, s // tile),
        in_specs=[spec(2 * tile), spec(2 * tile), spec(tile), spec(tile // L * SUBLANES)],
        out_specs=spec(tile),
        out_shape=jax.ShapeDtypeStruct((b, s, d), F32),
        scratch_shapes=[pltpu.VMEM((nb, d // gk, gk, gk), F32)],
        compiler_params=_params(("parallel", "arbitrary")),
        name="rwkv_scan",
    )(ar, bk, v, p_last)


def _rwkv_mixer_out(y_ref, bonus_ref, g_ref, lng_ref, lnb_ref, wo_ref, ones_ref):
    y = y_ref[...]
    inv_n = 1.0 / RWKV_HEAD
    mu = _headsum(y, ones_ref) * inv_n
    dy = y - mu
    var = _headsum(dy * dy, ones_ref) * inv_n
    yn = dy * lax.rsqrt(var + RWKV_GN_EPS) * lng_ref[...] + lnb_ref[...]
    return _dot(((yn + bonus_ref[...]) * g_ref[...]).astype(BF16), wo_ref[...])


def kernel(x, positions, ffn_norm, ffn_w_gate, ffn_w_up, ffn_w_down, mix_norm_even, w_in, conv_w, conv_b, conv_ln_g, conv_ln_b, q_norm, w_uq, kv_norm, w_ukv, w_out, mix_norm_odd, time_mu, w_r, w_k, w_v, w_o, w0, w1, w2, a0, a1, a2, g1, g2, k_k, k_a, r_k, ln_x_g, ln_x_b, final_norm):
    b, s, d = x.shape
    depth = ffn_norm.shape[0]
    x2 = x.reshape(b * s, d)
    wg, wu, wd = ffn_w_gate.astype(BF16), ffn_w_up.astype(BF16), ffn_w_down.astype(BF16)
    for layer in range(depth):
        x2 = _ffn(x2, ffn_norm[layer, 0], wg, wu, wd, (layer, 0))
        if layer % 2 == 0:
            e = layer // 2
            conv, q, k, v = _mix_in(x2.reshape(b, s, d), positions, mix_norm_even[e], w_in[e], conv_w[e], conv_b[e],
                                    conv_ln_g[e], conv_ln_b[e], q_norm[e], w_uq[e], kv_norm[e], w_ukv[e])
            attn = _attention(q, k, v)
            x2 = _mix_out(x2, conv.reshape(b * s, -1), attn.reshape(b * s, -1), w_out[e])
            mixer_out = None
        else:
            o = layer // 2
            ar, bk, v, p_last, bonus, g = _rwkv_in(
                x2.reshape(b, s, d), mix_norm_odd[o], time_mu[o], w_r[o], w_k[o], w_v[o], w0[o], w1[o], w2[o],
                a0[o], a1[o], a2[o], g1[o], g2[o], k_k[o], k_a[o], r_k[o].reshape(-1))
            y = _rwkv_scan(ar, bk, v, p_last)
            mixer_out = (y.reshape(b * s, d), bonus.reshape(b * s, d), g.reshape(b * s, d), ln_x_g[o], ln_x_b[o], w_o[o])
        last = layer == depth - 1
        x2 = _ffn(x2, ffn_norm[layer, 1], wg, wu, wd, (layer, 1), final_norm if last else None, mixer_out)
    return x2.reshape(b, s, d)
```

```python
import functools
import math

import jax
import jax.numpy as jnp
from jax import lax
from jax.experimental import pallas as pl
from jax.experimental.pallas import tpu as pltpu

F32 = jnp.float32
BF16 = jnp.bfloat16

NORM_EPS = 1e-6
FFN_RES_WEIGHT = 0.5
CONV_LN_EPS = 1e-5
MLA_HEADS = 8
QK_NOPE = 128
QK_ROPE = 64
V_HEAD = 128
QK_DIM = QK_NOPE + QK_ROPE
ROPE_THETA = 10000.0
RWKV_HEAD = 64
RWKV_GN_EPS = 64e-5

LANES = 128
SUBLANES = 8
MXU_TILE = 256
VMEM_LIMIT = 56 * 1024 * 1024
NEG_BIG = -1e30

SCAN_CHUNK = 64
SCAN_HEADS = 2


def _params(sem):
    return pltpu.CompilerParams(dimension_semantics=sem, vmem_limit_bytes=VMEM_LIMIT)


def _dot(a, b):
    return jnp.dot(a, b, preferred_element_type=F32)


def _dot_nt(a, b):
    return lax.dot_general(a, b, (((1,), (1,)), ((), ())), preferred_element_type=F32)


def _dot_tn(a, b):
    return lax.dot_general(a, b, (((0,), (0,)), ((), ())), preferred_element_type=F32)


def _rms(x, g, eps):
    ms = jnp.mean(x * x, axis=-1, keepdims=True)
    return x * lax.rsqrt(ms + eps) * g


def _hilo(x):
    hi = x.astype(BF16)
    lo = (x - hi.astype(F32)).astype(BF16)
    return hi, lo


def _const_spec(shape):
    nd = len(shape)
    return pl.BlockSpec(shape, lambda *_: (0,) * nd)


def _ffn_body(*refs, final, tf, rwkv_pre):
    refs = list(refs)
    x_ref = refs.pop(0)
    if rwkv_pre:
        pre_refs, refs = refs[:7], refs[7:]
    g_ref, wg_ref, wu_ref, wd_ref = refs[:4]
    fg_ref = refs[4] if final else None
    o_ref = refs[-1]
    x = x_ref[...]
    if rwkv_pre:
        x = x + _rwkv_mixer_out(*pre_refs)
    h = _rms(x, g_ref[...], NORM_EPS).astype(BF16)
    acc = None
    for c0 in range(0, wg_ref.shape[1], tf):
        gate = _dot(h, wg_ref[:, c0:c0 + tf])
        up = _dot(h, wu_ref[:, c0:c0 + tf])
        act = (gate * jax.nn.sigmoid(gate) * up).astype(BF16)
        part = _dot(act, wd_ref[c0:c0 + tf, :])
        acc = part if acc is None else acc + part
    y = x + FFN_RES_WEIGHT * acc
    if final:
        y = _rms(y, fg_ref[...], NORM_EPS)
    o_ref[...] = y


def _resident_spec(shape, lead):
    return pl.BlockSpec((None,) * len(lead) + shape, lambda *_: lead + (0,) * len(shape), pipeline_mode=pl.Buffered(1))


def _ffn(x2, g, wg, wu, wd, which, final_g=None, rwkv_pre=None, *, tm=1024, tf=256):
    t, d = x2.shape
    ff = wg.shape[-1]
    if rwkv_pre is not None:
        tm //= 2
    tm = min(tm, t)
    assert ff % tf == 0 and t % tm == 0
    final = final_g is not None
    tok = pl.BlockSpec((tm, d), lambda i: (i, 0))
    in_specs, args = [tok], [x2]
    if rwkv_pre is not None:
        y2, bonus2, gate2, ln_g, ln_b, w_o = rwkv_pre
        consts = [ln_g.reshape(1, d), ln_b.reshape(1, d), w_o.astype(BF16), _head_ones(d)]
        in_specs += [tok] * 3 + [_const_spec(c.shape) for c in consts]
        args += [y2, bonus2, gate2] + consts
    in_specs += [_const_spec((1, d)), _resident_spec((d, ff), which), _resident_spec((d, ff), which),
                 _resident_spec((ff, d), which)]
    args += [g.reshape(1, d), wg, wu, wd]
    if final:
        in_specs.append(_const_spec((1, d)))
        args.append(final_g.reshape(1, d))
    return pl.pallas_call(
        functools.partial(_ffn_body, final=final, tf=tf, rwkv_pre=rwkv_pre is not None),
        grid=(t // tm,),
        in_specs=in_specs,
        out_specs=tok,
        out_shape=jax.ShapeDtypeStruct((t, d), F32),
        compiler_params=_params(("parallel",)),
        name=("rwkv_out_" if rwkv_pre is not None else "") + ("ffn_final" if final else "ffn"),
    )(*args)


CONV_HALO = 32
CONV_ROWS = 64


def _mix_in_body(x_ref, pos_ref, g_ref, wa_ref, wgt_ref, wql_ref, wkvl_ref, wkpe_ref, wkpes_ref,
                 cw_ref, cb_ref, lng_ref, lnb_ref, qn_ref, wqn_ref, wqp_ref, wqps_ref,
                 kvn_ref, wukv_ref, invf_ref, sgn_ref,
                 conv_ref, q_ref, k_ref, v_ref, ext_scr, acc_scr, *, tm, width):
    si = pl.program_id(1)
    d_conv = ext_scr.shape[1]
    @pl.when(si == 0)
    def _():
        ext_scr[0:CONV_HALO, :] = jnp.zeros((CONV_HALO, d_conv), F32)

    @pl.when(si > 0)
    def _():
        ext_scr[0:CONV_HALO, :] = ext_scr[tm:tm + CONV_HALO, :]

    hn = _rms(x_ref[0], g_ref[...], NORM_EPS).astype(BF16)

    q_lat = _dot(hn, wql_ref[...])
    kv_lat = _dot(hn, wkvl_ref[...])
    k_pe_a = _dot(hn, wkpe_ref[...])
    k_pe_b = _dot(hn, wkpes_ref[...])

    za = _dot(hn, wa_ref[...])
    zg = _dot(hn, wgt_ref[...])
    ext_scr[CONV_HALO:CONV_HALO + tm, :] = za * jax.nn.sigmoid(zg)

    ang = pos_ref[0].astype(F32) * invf_ref[...]
    cos1 = jnp.cos(ang)
    sin1 = jnp.sin(ang) * sgn_ref[...]
    n_pe = MLA_HEADS * QK_ROPE
    cosq = jnp.concatenate([cos1] * (n_pe // LANES), axis=-1)
    sinq = jnp.concatenate([sin1] * (n_pe // LANES), axis=-1)

    qn = _rms(q_lat, qn_ref[...], NORM_EPS).astype(BF16)
    kvn = _rms(kv_lat, kvn_ref[...], NORM_EPS).astype(BF16)
    q_nope = _dot(qn, wqn_ref[...])
    q_pe = _dot(qn, wqp_ref[...]) * cosq + _dot(qn, wqps_ref[...]) * sinq
    kv = _dot(kvn, wukv_ref[...])
    k_pe = (k_pe_a * cos1 + k_pe_b * sin1)[:, 0:QK_ROPE].astype(BF16)

    for h in range(MLA_HEADS):
        q_ref[0, h, :, 0:QK_NOPE] = q_nope[:, h * QK_NOPE:(h + 1) * QK_NOPE].astype(BF16)
        q_ref[0, h, :, QK_NOPE:QK_DIM] = q_pe[:, h * QK_ROPE:(h + 1) * QK_ROPE].astype(BF16)
        c0 = h * (QK_NOPE + V_HEAD)
        k_ref[0, h, :, 0:QK_NOPE] = kv[:, c0:c0 + QK_NOPE].astype(BF16)
        k_ref[0, h, :, QK_NOPE:QK_DIM] = k_pe
        v_ref[0, h] = kv[:, c0 + QK_NOPE:c0 + QK_NOPE + V_HEAD].astype(BF16)

    base = CONV_HALO - (width - 1)

    def col_block(c, carry):
        cols = pl.ds(pl.multiple_of(c * LANES, LANES), LANES)
        w = cw_ref[:, cols]
        b = cb_ref[:, cols]
        for r0 in range(0, tm, CONV_ROWS):
            acc = jnp.broadcast_to(b, (CONV_ROWS, LANES))
            for rho in range(SUBLANES):
                taps = [j for j in range(width) if (base + j) % SUBLANES == rho]
                if not taps:
                    continue
                rows = CONV_ROWS + (SUBLANES if rho else 0)
                part = None
                for j in taps:
                    off = r0 + base + j - rho
                    term = w[j:j + 1, :] * ext_scr[off:off + rows, cols]
                    part = term if part is None else part + term
                acc = acc + part[rho:rho + CONV_ROWS, :]
            acc_scr[r0:r0 + CONV_ROWS, cols] = acc
        return carry

    lax.fori_loop(0, d_conv // LANES, col_block, 0)
    hc = acc_scr[...]
    mu = jnp.mean(hc, axis=-1, keepdims=True)
    dc = hc - mu
    var = jnp.mean(dc * dc, axis=-1, keepdims=True)
    hcn = dc * lax.rsqrt(var + CONV_LN_EPS) * lng_ref[...] + lnb_ref[...]
    conv_ref[0] = (hcn * jax.nn.sigmoid(hcn)).astype(BF16)


def _swap_halves(w, block):
    k, n = w.shape
    w = w.reshape(k, n // block, 2, block // 2)
    return w[:, :, ::-1, :].reshape(k, n)


def _mix_in(x, positions, norm_g, w_in, conv_w, conv_b, ln_g, ln_b, q_norm, w_uq, kv_norm, w_ukv, *, tm=512):
    b, s, d = x.shape
    tm = min(tm, s)
    width, d_conv = conv_w.shape
    q_lora = q_norm.shape[0]
    kv_lora = kv_norm.shape[0]
    assert width - 1 <= CONV_HALO and tm % CONV_ROWS == 0 and tm >= CONV_HALO
    o1, o2, o3 = 2 * d_conv, 2 * d_conv + q_lora, 2 * d_conv + q_lora + kv_lora
    w_a, w_gt = w_in[:, :d_conv].astype(BF16), w_in[:, d_conv:o1].astype(BF16)
    w_ql, w_kvl = w_in[:, o1:o2].astype(BF16), w_in[:, o2:o3].astype(BF16)
    w_kpe = w_in[:, o3:]
    pad = jnp.zeros((d, LANES - QK_ROPE), F32)
    w_kpes = jnp.concatenate([_swap_halves(w_kpe, QK_ROPE), pad], axis=1).astype(BF16)
    w_kpe = jnp.concatenate([w_kpe, pad], axis=1).astype(BF16)
    wq = w_uq.reshape(q_lora, MLA_HEADS, QK_DIM)
    w_qn = wq[:, :, :QK_NOPE].reshape(q_lora, MLA_HEADS * QK_NOPE).astype(BF16)
    w_qp = wq[:, :, QK_NOPE:].reshape(q_lora, MLA_HEADS * QK_ROPE)
    w_qps = _swap_halves(w_qp, QK_ROPE).astype(BF16)
    w_qp = w_qp.astype(BF16)
    inv_freq = 1.0 / (ROPE_THETA ** (jnp.arange(0, QK_ROPE, 2, dtype=F32) / QK_ROPE))
    invf = jnp.tile(inv_freq, 2 * LANES // QK_ROPE).reshape(1, LANES)
    half = QK_ROPE // 2
    sgn = jnp.tile(jnp.concatenate([-jnp.ones((half,), F32), jnp.ones((half,), F32)]), LANES // QK_ROPE).reshape(1, LANES)

    row = lambda v: v.reshape(1, -1)
    consts = [row(norm_g), w_a, w_gt, w_ql, w_kvl, w_kpe, w_kpes, conv_w, row(conv_b), row(ln_g), row(ln_b),
              row(q_norm), w_qn, w_qp, w_qps, row(kv_norm), w_ukv.astype(BF16), invf, sgn]
    in_specs = [pl.BlockSpec((1, tm, d), lambda bi, si: (bi, si, 0)),
                pl.BlockSpec((1, tm, 1), lambda bi, si: (bi, si, 0))]
    in_specs += [_const_spec(c.shape) for c in consts]
    hb = lambda bi, si: (bi, 0, si, 0)
    return pl.pallas_call(
        functools.partial(_mix_in_body, tm=tm, width=width),
        grid=(b, s // tm),
        in_specs=in_specs,
        out_specs=[pl.BlockSpec((1, tm, d_conv), lambda bi, si: (bi, si, 0)),
                   pl.BlockSpec((1, MLA_HEADS, tm, QK_DIM), hb),
                   pl.BlockSpec((1, MLA_HEADS, tm, QK_DIM), hb),
                   pl.BlockSpec((1, MLA_HEADS, tm, V_HEAD), hb)],
        out_shape=[jax.ShapeDtypeStruct((b, s, d_conv), BF16),
                   jax.ShapeDtypeStruct((b, MLA_HEADS, s, QK_DIM), BF16),
                   jax.ShapeDtypeStruct((b, MLA_HEADS, s, QK_DIM), BF16),
                   jax.ShapeDtypeStruct((b, MLA_HEADS, s, V_HEAD), BF16)],
        scratch_shapes=[pltpu.VMEM((tm + CONV_HALO, d_conv), F32), pltpu.VMEM((tm, d_conv), F32)],
        compiler_params=_params(("parallel", "arbitrary")),
        name="mix_in",
    )(x, positions.reshape(b, s, 1), *consts)


def _attn_body(q_ref, k_ref, v_ref, o_ref, *, tq, exp2_scale):
    seq = q_ref.shape[2]
    rows = lambda i: slice(i * tq, (i + 1) * tq)
    r = lax.broadcasted_iota(jnp.int32, (tq, tq), 0)
    c = lax.broadcasted_iota(jnp.int32, (tq, tq), 1)
    causal = r >= c

    def scores(qi, j):
        return _dot_nt(q_ref[0, 0, rows(qi), :], k_ref[0, 0, rows(j), :])

    pairs = [(qi, j) for qi in range(seq // tq) for j in range(qi + 1)]
    s_next = scores(*pairs[0])
    for idx, (qi, j) in enumerate(pairs):
        s = s_next
        if idx + 1 < len(pairs):
            s_next = scores(*pairs[idx + 1])
        if j == 0:
            m = jnp.full((tq, 1), NEG_BIG, F32)
            l = jnp.zeros((tq, 1), F32)
            acc = jnp.zeros((tq, V_HEAD), F32)
        if j == qi:
            s = jnp.where(causal, s, NEG_BIG)
        m_new = jnp.maximum(m, jnp.max(s, axis=-1, keepdims=True))
        p = jnp.exp2((s - m_new) * exp2_scale)
        alpha = jnp.exp2((m - m_new) * exp2_scale)
        l = alpha * l + jnp.sum(p, axis=-1, keepdims=True)
        acc = alpha * acc + _dot(p.astype(BF16), v_ref[0, 0, rows(j), :])
        m = m_new
        if j == qi:
            o_ref[0, rows(qi), :] = (acc / l).astype(BF16)


def _attention(q, k, v, *, tq=512):
    b, h, s, _ = q.shape
    tq = min(tq, s)
    assert s % tq == 0
    exp2_scale = (QK_DIM ** -0.5) * math.log2(math.e)
    head = lambda d: pl.BlockSpec((1, 1, s, d), lambda bi, hi: (bi, hi, 0, 0))
    return pl.pallas_call(
        functools.partial(_attn_body, tq=tq, exp2_scale=exp2_scale),
        grid=(b, h),
        in_specs=[head(QK_DIM), head(QK_DIM), head(V_HEAD)],
        out_specs=pl.BlockSpec((1, s, V_HEAD), lambda bi, hi: (bi, 0, hi)),
        out_shape=jax.ShapeDtypeStruct((b, s, h * V_HEAD), BF16),
        compiler_params=_params(("parallel", "parallel")),
        name="attention",
    )(q, k, v)


def _mix_out_body(x_ref, c_ref, a_ref, wc_ref, wa_ref, o_ref):
    o_ref[...] = x_ref[...] + _dot(c_ref[...], wc_ref[...]) + _dot(a_ref[...], wa_ref[...])


def _mix_out(x2, conv2, attn2, w_out, *, tm=1024):
    t, d = x2.shape
    tm = min(tm, t)
    dc, da = conv2.shape[1], attn2.shape[1]
    wc, wa = w_out[:dc].astype(BF16), w_out[dc:].astype(BF16)
    return pl.pallas_call(
        _mix_out_body,
        grid=(t // tm,),
        in_specs=[pl.BlockSpec((tm, d), lambda i: (i, 0)),
                  pl.BlockSpec((tm, dc), lambda i: (i, 0)),
                  pl.BlockSpec((tm, da), lambda i: (i, 0)),
                  _const_spec(wc.shape), _const_spec(wa.shape)],
        out_specs=pl.BlockSpec((tm, d), lambda i: (i, 0)),
        out_shape=jax.ShapeDtypeStruct((t, d), F32),
        compiler_params=_params(("parallel",)),
        name="mix_out",
    )(x2, conv2, attn2, wc, wa)


SHIFT_HALO = 8


def _headsum(x, ones_ref):
    w = ones_ref.shape[0]
    hi, lo = _hilo(x)
    blocks = [_dot(hi[:, c:c + w], ones_ref[...]) + _dot(lo[:, c:c + w], ones_ref[...]) for c in range(0, x.shape[1], w)]
    return jnp.concatenate(blocks, axis=1)


def _rwkv_in_body(x_ref, g_ref, mu_ref, wr_ref, wk_ref, wv_ref, w0_ref, w1_ref, w2_ref, a0_ref, a1_ref, a2_ref,
                  g1_ref, g2_ref, kk_ref, ka_ref, rk_ref, ones_ref, tri_ref,
                  ar_out, bk_out, v_out, pl_out, bonus_out, g_out, carry_scr, *, tm):
    si = pl.program_id(1)
    L = SCAN_CHUNK
    d = carry_scr.shape[1]
    h = _rms(x_ref[0], g_ref[...], NORM_EPS)

    @pl.when(si == 0)
    def _():
        carry_scr[...] = jnp.zeros_like(carry_scr)

    prev_last = carry_scr[SHIFT_HALO - 1:SHIFT_HALO, :]
    first_row = lax.broadcasted_iota(jnp.int32, (tm, d), 0) == 0
    hh = jnp.where(first_row, prev_last, pltpu.roll(h, 1, axis=0)) - h
    carry_scr[...] = h[tm - SHIFT_HALO:tm, :]
    mix = lambda i: (h + hh * mu_ref[i:i + 1, :]).astype(BF16)
    xr, xw, xk, xv, xa, xg = [mix(i) for i in range(6)]
    lora_w = _dot(xw, w1_ref[...])
    lora_a = _dot(xa, a1_ref[...])
    lora_g = _dot(xg, g1_ref[...])
    r = _dot(xr, wr_ref[...])
    k = _dot(xk, wk_ref[...])
    v = _dot(xv, wv_ref[...])
    z = w0_ref[...] + _dot(jnp.tanh(lora_w).astype(BF16), w2_ref[...])
    lw = -math.exp(-0.5) * jax.nn.sigmoid(z)
    a = jax.nn.sigmoid(a0_ref[...] + _dot(lora_a.astype(BF16), a2_ref[...]))
    g = _dot(jax.nn.sigmoid(lora_g).astype(BF16), g2_ref[...])
    kkr = k * kk_ref[...]
    kk = kkr * lax.rsqrt(jnp.maximum(_headsum(kkr * kkr, ones_ref), 1e-24))
    k2 = k * (1.0 + (a - 1.0) * ka_ref[...])
    bonus_out[0] = _headsum(r * k2 * rk_ref[...], ones_ref) * v
    g_out[0] = g
    v_out[0] = v.astype(BF16)

    lw_hi, lw_lo = _hilo(lw)
    cum = _dot(tri_ref[...], lw_hi) + _dot(tri_ref[...], lw_lo)
    e_pos = jnp.exp(cum)
    e_neg = jnp.exp(-cum)
    at = (-kk * jnp.exp(cum - lw)).astype(BF16)
    rt = (r * e_pos).astype(BF16)
    bt = (kk * a * e_neg).astype(BF16)
    kt = (k2 * e_neg).astype(BF16)
    for c in range(tm // L):
        tok = slice(c * L, (c + 1) * L)
        ar_out[0, 2 * c * L:(2 * c + 1) * L, :] = at[tok]
        ar_out[0, (2 * c + 1) * L:(2 * c + 2) * L, :] = rt[tok]
        bk_out[0, 2 * c * L:(2 * c + 1) * L, :] = bt[tok]
        bk_out[0, (2 * c + 1) * L:(2 * c + 2) * L, :] = kt[tok]
        pl_out[0, c * SUBLANES:(c + 1) * SUBLANES, :] = jnp.broadcast_to(e_pos[(c + 1) * L - 1:(c + 1) * L, :], (SUBLANES, d))


def _pad_cols(w, n):
    return jnp.pad(w, ((0, 0), (0, n - w.shape[1])))


def _pad_rows(w, n):
    return jnp.pad(w, ((0, n - w.shape[0]), (0, 0)))


def _head_ones(d):
    w = min(MXU_TILE, d)
    assert d % w == 0 and w % RWKV_HEAD == 0
    idx = jnp.arange(w) // RWKV_HEAD
    return (idx[:, None] == idx[None, :]).astype(BF16)


def _rwkv_in(x, norm_g, time_mu, w_r, w_k, w_v, w0, w1, w2, a0, a1, a2, g1, g2, k_k, k_a, r_k, *, tm=512):
    b, s, d = x.shape
    tm = min(tm, s)
    L = SCAN_CHUNK
    assert tm % L == 0 and s % tm == 0
    row = lambda v: v.reshape(1, -1)
    lp = lambda n: ((n + LANES - 1) // LANES) * LANES
    idx = jnp.arange(tm)
    tri = ((idx[:, None] // L == idx[None, :] // L) & (idx[None, :] <= idx[:, None])).astype(BF16)
    consts = [row(norm_g), time_mu, w_r.astype(BF16), w_k.astype(BF16), w_v.astype(BF16),
              row(w0), _pad_cols(w1, lp(w1.shape[1])).astype(BF16), _pad_rows(w2, lp(w2.shape[0])).astype(BF16),
              row(a0), _pad_cols(a1, lp(a1.shape[1])).astype(BF16), _pad_rows(a2, lp(a2.shape[0])).astype(BF16),
              _pad_cols(g1, lp(g1.shape[1])).astype(BF16), _pad_rows(g2, lp(g2.shape[0])).astype(BF16),
              row(k_k), row(k_a), row(r_k), _head_ones(d), tri]
    tok = lambda rows: pl.BlockSpec((1, rows, d), lambda bi, si: (bi, si, 0))
    pl_rows = tm // L * SUBLANES
    return pl.pallas_call(
        functools.partial(_rwkv_in_body, tm=tm),
        grid=(b, s // tm),
        in_specs=[tok(tm)] + [_const_spec(c.shape) for c in consts],
        out_specs=[tok(2 * tm), tok(2 * tm), tok(tm), tok(pl_rows), tok(tm), tok(tm)],
        out_shape=[jax.ShapeDtypeStruct((b, 2 * s, d), BF16), jax.ShapeDtypeStruct((b, 2 * s, d), BF16),
                   jax.ShapeDtypeStruct((b, s, d), BF16), jax.ShapeDtypeStruct((b, s // L * SUBLANES, d), F32),
                   jax.ShapeDtypeStruct((b, s, d), F32), jax.ShapeDtypeStruct((b, s, d), F32)],
        scratch_shapes=[pltpu.VMEM((SHIFT_HALO, d), F32)],
        compiler_params=_params(("parallel", "arbitrary")),
        name="rwkv_in",
    )(x, *consts)


def _rwkv_scan_body(ar_ref, bk_ref, v_ref, pl_ref, y_ref, ht_scr, *, tile, heads):
    L = SCAN_CHUNK
    gk = heads * RWKV_HEAD
    gl = heads * L
    nb, groups = ht_scr.shape[0], ht_scr.shape[1]
    chains = [(bi, g) for bi in range(nb) for g in range(groups)]
    row = lax.broadcasted_iota(jnp.int32, (L, gl), 0)
    pos = lax.broadcasted_iota(jnp.int32, (L, gl), 1) % L
    strict = pos < row
    incl = pos <= row
    blk = lambda shape: (lax.broadcasted_iota(jnp.int32, shape, 0) // L) % heads == lax.broadcasted_iota(jnp.int32, shape, 1) // L
    bd_mask = blk((gl, gk))
    bd2_mask = blk((2 * gl, gk))

    @pl.when(pl.program_id(1) == 0)
    def _():
        ht_scr[...] = jnp.zeros_like(ht_scr)

    def bd(x):
        return jnp.where(bd_mask, jnp.concatenate([x] * heads, axis=0), 0).astype(BF16)

    def chunk(c, carry):
        each = lambda f, *xs: [f(*a) for a in zip(*xs)]
        cat = lambda *xs: jnp.concatenate(xs, axis=0)
        load = lambda ref, n: [ref[bi, pl.ds(pl.multiple_of(c * n, n), n), g * gk:(g + 1) * gk] for bi, g in chains]
        lhs = load(ar_ref, 2 * L)
        bk = load(bk_ref, 2 * L)
        v = load(v_ref, L)
        p_last = [x[0:1, :] for x in load(pl_ref, SUBLANES)]
        ht = [ht_scr[bi, g] for bi, g in chains]
        abk = each(lambda l_, x: _dot_nt(l_, jnp.where(bd2_mask, cat(*([x[:L]] * heads + [x[L:]] * heads)), 0)), lhs, bk)
        xr = each(lambda l_, h_: _dot_nt(l_, h_.astype(BF16)), lhs, ht)
        a_ab = [jnp.where(strict, x[:L, :gl], 0.0) for x in abk]
        a_rb = [jnp.where(incl, x[L:, :gl], 0.0) for x in abk]
        av = each(lambda x, v_: _dot(cat(jnp.where(strict, x[:L, gl:], 0.0), jnp.where(incl, x[L:, gl:], 0.0)).astype(BF16), bd(v_)),
                  abk, v)
        x2 = each(lambda x, a_: x[:L] + a_[:L], xr, av)
        step = lambda p_, x_: _dot(p_.astype(BF16), jnp.concatenate([bd(p_), bd(x_)], axis=1))
        res = each(step, a_ab, x2)
        rounds = int(math.log2(L)) - 1
        for i in range(rounds):
            p = [x[:, :gl] for x in res]
            x2 = each(lambda x_, r_: x_ + r_[:, gl:], x2, res)
            if i < rounds - 1:
                res = each(step, p, x2)
        u = each(lambda x_, p_: x_ + _dot(p_.astype(BF16), bd(x_)), x2, p)
        yu = each(lambda a_, u_: _dot(a_.astype(BF16), bd(u_)), a_rb, u)
        upd = each(lambda u_, v_, x: _dot_tn(cat(u_.astype(BF16), v_), x), u, v, bk)
        for i, (bi, g) in enumerate(chains):
            y_ref[bi, pl.ds(pl.multiple_of(c * L, L), L), g * gk:(g + 1) * gk] = xr[i][L:] + av[i][L:] + yu[i]
            ht_scr[bi, g] = (ht[i] + jnp.where(bd_mask, upd[i], 0.0)) * p_last[i]
        return carry

    lax.fori_loop(0, tile // L, chunk, 0)


def _rwkv_scan(ar, bk, v, p_last, *, nb=4, tile=256):
    b, s, d = v.shape
    gk = SCAN_HEADS * RWKV_HEAD
    L = SCAN_CHUNK
    tile = min(tile, s)
    nb = min(nb, b)
    assert L == RWKV_HEAD and tile % L == 0 and s % tile == 0 and d % gk == 0 and b % nb == 0
    spec = lambda rows: pl.BlockSpec((nb, rows, d), lambda bi, si: (bi, si, 0))
    return pl.pallas_call(
        functools.partial(_rwkv_scan_body, tile=tile, heads=SCAN_HEADS),
        grid=(b // nb, s // tile),
        in_specs=[spec(2 * tile), spec(2 * tile), spec(tile), spec(tile // L * SUBLANES)],
        out_specs=spec(tile),
        out_shape=jax.ShapeDtypeStruct((b, s, d), F32),
        scratch_shapes=[pltpu.VMEM((nb, d // gk, gk, gk), F32)],
        compiler_params=_params(("parallel", "arbitrary")),
        name="rwkv_scan",
    )(ar, bk, v, p_last)


def _rwkv_mixer_out(y_ref, bonus_ref, g_ref, lng_ref, lnb_ref, wo_ref, ones_ref):
    y = y_ref[...]
    inv_n = 1.0 / RWKV_HEAD
    mu = _headsum(y, ones_ref) * inv_n
    dy = y - mu
    var = _headsum(dy * dy, ones_ref) * inv_n
    yn = dy * lax.rsqrt(var + RWKV_GN_EPS) * lng_ref[...] + lnb_ref[...]
    return _dot(((yn + bonus_ref[...]) * g_ref[...]).astype(BF16), wo_ref[...])


def kernel(x, positions, ffn_norm, ffn_w_gate, ffn_w_up, ffn_w_down, mix_norm_even, w_in, conv_w, conv_b, conv_ln_g, conv_ln_b, q_norm, w_uq, kv_norm, w_ukv, w_out, mix_norm_odd, time_mu, w_r, w_k, w_v, w_o, w0, w1, w2, a0, a1, a2, g1, g2, k_k, k_a, r_k, ln_x_g, ln_x_b, final_norm):
    b, s, d = x.shape
    depth = ffn_norm.shape[0]
    x2 = x.reshape(b * s, d)
    wg, wu, wd = ffn_w_gate.astype(BF16), ffn_w_up.astype(BF16), ffn_w_down.astype(BF16)
    for layer in range(depth):
        x2 = _ffn(x2, ffn_norm[layer, 0], wg, wu, wd, (layer, 0))
        if layer % 2 == 0:
            e = layer // 2
            conv, q, k, v = _mix_in(x2.reshape(b, s, d), positions, mix_norm_even[e], w_in[e], conv_w[e], conv_b[e],
                                    conv_ln_g[e], conv_ln_b[e], q_norm[e], w_uq[e], kv_norm[e], w_ukv[e])
            attn = _attention(q, k, v)
            x2 = _mix_out(x2, conv.reshape(b * s, -1), attn.reshape(b * s, -1), w_out[e])
            mixer_out = None
        else:
            o = layer // 2
            ar, bk, v, p_last, bonus, g = _rwkv_in(
                x2.reshape(b, s, d), mix_norm_odd[o], time_mu[o], w_r[o], w_k[o], w_v[o], w0[o], w1[o], w2[o],
                a0[o], a1[o], a2[o], g1[o], g2[o], k_k[o], k_a[o], r_k[o].reshape(-1))
            y = _rwkv_scan(ar, bk, v, p_last)
            mixer_out = (y.reshape(b * s, d), bonus.reshape(b * s, d), g.reshape(b * s, d), ln_x_g[o], ln_x_b[o], w_o[o])
        last = layer == depth - 1
        x2 = _ffn(x2, ffn_norm[layer, 1], wg, wu, wd, (layer, 1), final_norm if last else None, mixer_out)
    return x2.reshape(b, s, d)
```

```python
import functools
import math

import jax
import jax.numpy as jnp
from jax import lax
from jax.experimental import pallas as pl
from jax.experimental.pallas import tpu as pltpu

F32 = jnp.float32
BF16 = jnp.bfloat16

NORM_EPS = 1e-6
FFN_RES_WEIGHT = 0.5
CONV_LN_EPS = 1e-5
MLA_HEADS = 8
QK_NOPE = 128
QK_ROPE = 64
V_HEAD = 128
QK_DIM = QK_NOPE + QK_ROPE
ROPE_THETA = 10000.0
ATTN_EXP2_SCALE = (QK_DIM ** -0.5) * math.log2(math.e)
RWKV_HEAD = 64
RWKV_GN_EPS = 64e-5

LANES = 128
SUBLANES = 8
MXU_TILE = 256
VMEM_LIMIT = 56 * 1024 * 1024
NEG_BIG = -1e30

SCAN_CHUNK = 64
SCAN_HEADS = 2


def _params(sem):
    return pltpu.CompilerParams(dimension_semantics=sem, vmem_limit_bytes=VMEM_LIMIT)


def _dot(a, b):
    return jnp.dot(a, b, preferred_element_type=F32)


def _dot_nt(a, b):
    return lax.dot_general(a, b, (((1,), (1,)), ((), ())), preferred_element_type=F32)


def _dot_tn(a, b):
    return lax.dot_general(a, b, (((0,), (0,)), ((), ())), preferred_element_type=F32)


def _rms(x, g, eps):
    ms = jnp.mean(x * x, axis=-1, keepdims=True)
    return x * lax.rsqrt(ms + eps) * g


def _hilo(x):
    hi = x.astype(BF16)
    lo = (x - hi.astype(F32)).astype(BF16)
    return hi, lo


def _const_spec(shape):
    nd = len(shape)
    return pl.BlockSpec(shape, lambda *_: (0,) * nd)


def _ffn_body(*refs, final, tf, rwkv_pre):
    refs = list(refs)
    x_ref = refs.pop(0)
    if rwkv_pre:
        pre_refs, refs = refs[:7], refs[7:]
    g_ref, wg_ref, wu_ref, wd_ref = refs[:4]
    fg_ref = refs[4] if final else None
    o_ref = refs[-1]
    x = x_ref[...]
    if rwkv_pre:
        x = x + _rwkv_mixer_out(*pre_refs)
    h = _rms(x, g_ref[...], NORM_EPS).astype(BF16)
    acc = None
    for c0 in range(0, wg_ref.shape[1], tf):
        gate = _dot(h, wg_ref[:, c0:c0 + tf])
        up = _dot(h, wu_ref[:, c0:c0 + tf])
        act = (gate * jax.nn.sigmoid(gate) * up).astype(BF16)
        part = _dot(act, wd_ref[c0:c0 + tf, :])
        acc = part if acc is None else acc + part
    y = x + FFN_RES_WEIGHT * acc
    if final:
        y = _rms(y, fg_ref[...], NORM_EPS)
    o_ref[...] = y


def _resident_spec(shape, lead):
    return pl.BlockSpec((None,) * len(lead) + shape, lambda *_: lead + (0,) * len(shape), pipeline_mode=pl.Buffered(1))


def _ffn(x2, g, wg, wu, wd, which, final_g=None, rwkv_pre=None, *, tm=1024, tf=256):
    t, d = x2.shape
    ff = wg.shape[-1]
    if rwkv_pre is not None:
        tm //= 2
    tm = min(tm, t)
    assert ff % tf == 0 and t % tm == 0
    final = final_g is not None
    tok = pl.BlockSpec((tm, d), lambda i: (i, 0))
    in_specs, args = [tok], [x2]
    if rwkv_pre is not None:
        y2, bonus2, gate2, ln_g, ln_b, w_o = rwkv_pre
        consts = [ln_g.reshape(1, d), ln_b.reshape(1, d), w_o.astype(BF16), _head_ones(d)]
        in_specs += [tok] * 3 + [_const_spec(c.shape) for c in consts]
        args += [y2, bonus2, gate2] + consts
    in_specs += [_const_spec((1, d)), _resident_spec((d, ff), which), _resident_spec((d, ff), which),
                 _resident_spec((ff, d), which)]
    args += [g.reshape(1, d), wg, wu, wd]
    if final:
        in_specs.append(_const_spec((1, d)))
        args.append(final_g.reshape(1, d))
    return pl.pallas_call(
        functools.partial(_ffn_body, final=final, tf=tf, rwkv_pre=rwkv_pre is not None),
        grid=(t // tm,),
        in_specs=in_specs,
        out_specs=tok,
        out_shape=jax.ShapeDtypeStruct((t, d), F32),
        compiler_params=_params(("parallel",)),
        name=("rwkv_out_" if rwkv_pre is not None else "") + ("ffn_final" if final else "ffn"),
    )(*args)


CONV_HALO = 32
CONV_ROWS = 64


def _mix_in_body(x_ref, pos_ref, g_ref, wa_ref, wgt_ref, wql_ref, wkvl_ref, wkpe_ref, wkpes_ref,
                 cw_ref, cb_ref, lng_ref, lnb_ref, qn_ref, wqn_ref, wqp_ref, wqps_ref,
                 kvn_ref, wukv_ref, invf_ref, sgn_ref,
                 conv_ref, q_ref, k_ref, v_ref, ext_scr, acc_scr, *, tm, width):
    si = pl.program_id(1)
    d_conv = ext_scr.shape[1]
    @pl.when(si == 0)
    def _():
        ext_scr[0:CONV_HALO, :] = jnp.zeros((CONV_HALO, d_conv), F32)

    @pl.when(si > 0)
    def _():
        ext_scr[0:CONV_HALO, :] = ext_scr[tm:tm + CONV_HALO, :]

    hn = _rms(x_ref[0], g_ref[...], NORM_EPS).astype(BF16)

    q_lat = _dot(hn, wql_ref[...])
    kv_lat = _dot(hn, wkvl_ref[...])
    k_pe_a = _dot(hn, wkpe_ref[...])
    k_pe_b = _dot(hn, wkpes_ref[...])

    za = _dot(hn, wa_ref[...])
    zg = _dot(hn, wgt_ref[...])
    ext_scr[CONV_HALO:CONV_HALO + tm, :] = za * jax.nn.sigmoid(zg)

    ang = pos_ref[0].astype(F32) * invf_ref[...]
    cos1 = jnp.cos(ang)
    sin1 = jnp.sin(ang) * sgn_ref[...]
    n_pe = MLA_HEADS * QK_ROPE
    cosq = jnp.concatenate([cos1] * (n_pe // LANES), axis=-1)
    sinq = jnp.concatenate([sin1] * (n_pe // LANES), axis=-1)

    qn = _rms(q_lat, qn_ref[...], NORM_EPS).astype(BF16)
    kvn = _rms(kv_lat, kvn_ref[...], NORM_EPS).astype(BF16)
    q_nope = _dot(qn, wqn_ref[...])
    q_pe = _dot(qn, wqp_ref[...]) * cosq + _dot(qn, wqps_ref[...]) * sinq
    kv = _dot(kvn, wukv_ref[...])
    k_pe = (k_pe_a * cos1 + k_pe_b * sin1)[:, 0:QK_ROPE].astype(BF16)

    for h in range(MLA_HEADS):
        q_ref[0, h, :, 0:QK_NOPE] = q_nope[:, h * QK_NOPE:(h + 1) * QK_NOPE].astype(BF16)
        q_ref[0, h, :, QK_NOPE:QK_DIM] = q_pe[:, h * QK_ROPE:(h + 1) * QK_ROPE].astype(BF16)
        c0 = h * (QK_NOPE + V_HEAD)
        k_ref[0, h, :, 0:QK_NOPE] = kv[:, c0:c0 + QK_NOPE].astype(BF16)
        k_ref[0, h, :, QK_NOPE:QK_DIM] = k_pe
        v_ref[0, h] = kv[:, c0 + QK_NOPE:c0 + QK_NOPE + V_HEAD].astype(BF16)

    base = CONV_HALO - (width - 1)

    def col_block(c, carry):
        cols = pl.ds(pl.multiple_of(c * LANES, LANES), LANES)
        w = cw_ref[:, cols]
        b = cb_ref[:, cols]
        for r0 in range(0, tm, CONV_ROWS):
            acc = jnp.broadcast_to(b, (CONV_ROWS, LANES))
            for rho in range(SUBLANES):
                taps = [j for j in range(width) if (base + j) % SUBLANES == rho]
                if not taps:
                    continue
                rows = CONV_ROWS + (SUBLANES if rho else 0)
                part = None
                for j in taps:
                    off = r0 + base + j - rho
                    term = w[j:j + 1, :] * ext_scr[off:off + rows, cols]
                    part = term if part is None else part + term
                acc = acc + part[rho:rho + CONV_ROWS, :]
            acc_scr[r0:r0 + CONV_ROWS, cols] = acc
        return carry

    lax.fori_loop(0, d_conv // LANES, col_block, 0)
    hc = acc_scr[...]
    mu = jnp.mean(hc, axis=-1, keepdims=True)
    dc = hc - mu
    var = jnp.mean(dc * dc, axis=-1, keepdims=True)
    hcn = dc * lax.rsqrt(var + CONV_LN_EPS) * lng_ref[...] + lnb_ref[...]
    conv_ref[0] = (hcn * jax.nn.sigmoid(hcn)).astype(BF16)


def _swap_halves(w, block):
    k, n = w.shape
    w = w.reshape(k, n // block, 2, block // 2)
    return w[:, :, ::-1, :].reshape(k, n)


def _mix_in(x, positions, norm_g, w_in, conv_w, conv_b, ln_g, ln_b, q_norm, w_uq, kv_norm, w_ukv, *, tm=512):
    b, s, d = x.shape
    tm = min(tm, s)
    width, d_conv = conv_w.shape
    q_lora = q_norm.shape[0]
    kv_lora = kv_norm.shape[0]
    assert width - 1 <= CONV_HALO and tm % CONV_ROWS == 0 and tm >= CONV_HALO
    o1, o2, o3 = 2 * d_conv, 2 * d_conv + q_lora, 2 * d_conv + q_lora + kv_lora
    w_a, w_gt = w_in[:, :d_conv].astype(BF16), w_in[:, d_conv:o1].astype(BF16)
    w_ql, w_kvl = w_in[:, o1:o2].astype(BF16), w_in[:, o2:o3].astype(BF16)
    w_kpe = w_in[:, o3:]
    pad = jnp.zeros((d, LANES - QK_ROPE), F32)
    w_kpes = jnp.concatenate([_swap_halves(w_kpe, QK_ROPE), pad], axis=1).astype(BF16)
    w_kpe = jnp.concatenate([w_kpe, pad], axis=1).astype(BF16)
    wq = (w_uq * ATTN_EXP2_SCALE).reshape(q_lora, MLA_HEADS, QK_DIM)
    w_qn =wq[:, :, :QK_NOPE].reshape(q_lora, MLA_HEADS * QK_NOPE).astype(BF16)
    w_qp = wq[:, :, QK_NOPE:].reshape(q_lora, MLA_HEADS * QK_ROPE)
    w_qps = _swap_halves(w_qp, QK_ROPE).astype(BF16)
    w_qp = w_qp.astype(BF16)
    inv_freq = 1.0 / (ROPE_THETA ** (jnp.arange(0, QK_ROPE, 2, dtype=F32) / QK_ROPE))
    invf = jnp.tile(inv_freq, 2 * LANES // QK_ROPE).reshape(1, LANES)
    half = QK_ROPE // 2
    sgn = jnp.tile(jnp.concatenate([-jnp.ones((half,), F32), jnp.ones((half,), F32)]), LANES // QK_ROPE).reshape(1, LANES)

    row = lambda v: v.reshape(1, -1)
    consts = [row(norm_g), w_a, w_gt, w_ql, w_kvl, w_kpe, w_kpes, conv_w, row(conv_b), row(ln_g), row(ln_b),
              row(q_norm), w_qn, w_qp, w_qps, row(kv_norm), w_ukv.astype(BF16), invf, sgn]
    in_specs = [pl.BlockSpec((1, tm, d), lambda bi, si: (bi, si, 0)),
                pl.BlockSpec((1, tm, 1), lambda bi, si: (bi, si, 0))]
    in_specs += [_const_spec(c.shape) for c in consts]
    hb = lambda bi, si: (bi, 0, si, 0)
    return pl.pallas_call(
        functools.partial(_mix_in_body, tm=tm, width=width),
        grid=(b, s // tm),
        in_specs=in_specs,
        out_specs=[pl.BlockSpec((1, tm, d_conv), lambda bi, si: (bi, si, 0)),
                   pl.BlockSpec((1, MLA_HEADS, tm, QK_DIM), hb),
                   pl.BlockSpec((1, MLA_HEADS, tm, QK_DIM), hb),
                   pl.BlockSpec((1, MLA_HEADS, tm, V_HEAD), hb)],
        out_shape=[jax.ShapeDtypeStruct((b, s, d_conv), BF16),
                   jax.ShapeDtypeStruct((b, MLA_HEADS, s, QK_DIM), BF16),
                   jax.ShapeDtypeStruct((b, MLA_HEADS, s, QK_DIM), BF16),
                   jax.ShapeDtypeStruct((b, MLA_HEADS, s, V_HEAD), BF16)],
        scratch_shapes=[pltpu.VMEM((tm + CONV_HALO, d_conv), F32), pltpu.VMEM((tm, d_conv), F32)],
        compiler_params=_params(("parallel", "arbitrary")),
        name="mix_in",
    )(x, positions.reshape(b, s, 1), *consts)


def _attn_body(q_ref, k_ref, v_ref, o_ref, *, tq):
    seq = q_ref.shape[2]
    rows = lambda i: slice(i * tq, (i + 1) * tq)
    r = lax.broadcasted_iota(jnp.int32, (tq, tq), 0)
    c = lax.broadcasted_iota(jnp.int32, (tq, tq), 1)
    causal = r >= c

    def scores(qi, j):
        return _dot_nt(q_ref[0, 0, rows(qi), :], k_ref[0, 0, rows(j), :])

    pairs = [(qi, j) for qi in range(seq // tq) for j in range(qi + 1)]
    s_next = scores(*pairs[0])
    for idx, (qi, j) in enumerate(pairs):
        s = s_next
        if idx + 1 < len(pairs):
            s_next = scores(*pairs[idx + 1])
        if j == 0:
            m = jnp.full((tq, 1), NEG_BIG, F32)
            l = jnp.zeros((tq, 1), F32)
            acc = jnp.zeros((tq, V_HEAD), F32)
        if j == qi:
            s = jnp.where(causal, s, NEG_BIG)
        m_new = jnp.maximum(m, jnp.max(s, axis=-1, keepdims=True))
        p = jnp.exp2(s - m_new)
        alpha = jnp.exp2(m - m_new)
        l = alpha * l + jnp.sum(p, axis=-1, keepdims=True)
        acc = alpha * acc + _dot(p.astype(BF16), v_ref[0, 0, rows(j), :])
        m = m_new
        if j == qi:
            o_ref[0, rows(qi), :] = (acc / l).astype(BF16)


def _attention(q, k, v, *, tq=512):
    b, h, s, _ = q.shape
    tq = min(tq, s)
    assert s % tq == 0
    head =lambda d: pl.BlockSpec((1, 1, s, d), lambda bi, hi: (bi, hi, 0, 0))
    return pl.pallas_call(
        functools.partial(_attn_body, tq=tq),
        grid=(b, h),
        in_specs=[head(QK_DIM), head(QK_DIM), head(V_HEAD)],
        out_specs=pl.BlockSpec((1, s, V_HEAD), lambda bi, hi: (bi, 0, hi)),
        out_shape=jax.ShapeDtypeStruct((b, s, h * V_HEAD), BF16),
        compiler_params=_params(("parallel", "parallel")),
        name="attention",
    )(q, k, v)


def _mix_out_body(x_ref, c_ref, a_ref, wc_ref, wa_ref, o_ref):
    o_ref[...] = x_ref[...] + _dot(c_ref[...], wc_ref[...]) + _dot(a_ref[...], wa_ref[...])


def _mix_out(x2, conv2, attn2, w_out, *, tm=1024):
    t, d = x2.shape
    tm = min(tm, t)
    dc, da = conv2.shape[1], attn2.shape[1]
    wc, wa = w_out[:dc].astype(BF16), w_out[dc:].astype(BF16)
    return pl.pallas_call(
        _mix_out_body,
        grid=(t // tm,),
        in_specs=[pl.BlockSpec((tm, d), lambda i: (i, 0)),
                  pl.BlockSpec((tm, dc), lambda i: (i, 0)),
                  pl.BlockSpec((tm, da), lambda i: (i, 0)),
                  _const_spec(wc.shape), _const_spec(wa.shape)],
        out_specs=pl.BlockSpec((tm, d), lambda i: (i, 0)),
        out_shape=jax.ShapeDtypeStruct((t, d), F32),
        compiler_params=_params(("parallel",)),
        name="mix_out",
    )(x2, conv2, attn2, wc, wa)


SHIFT_HALO = 8


def _headsum(x, ones_ref):
    w = ones_ref.shape[0]
    hi, lo = _hilo(x)
    blocks = [_dot(hi[:, c:c + w], ones_ref[...]) + _dot(lo[:, c:c + w], ones_ref[...]) for c in range(0, x.shape[1], w)]
    return jnp.concatenate(blocks, axis=1)


def _rwkv_in_body(x_ref, g_ref, mu_ref, wr_ref, wk_ref, wv_ref, w0_ref, w1_ref, w2_ref, a0_ref, a1_ref, a2_ref,
                  g1_ref, g2_ref, kk_ref, ka_ref, rk_ref, ones_ref, tri_ref,
                  ar_out, bk_out, v_out, pl_out, bonus_out, g_out, carry_scr, *, tm):
    si = pl.program_id(1)
    L = SCAN_CHUNK
    d = carry_scr.shape[1]
    h = _rms(x_ref[0], g_ref[...], NORM_EPS)

    @pl.when(si == 0)
    def _():
        carry_scr[...] = jnp.zeros_like(carry_scr)

    prev_last = carry_scr[SHIFT_HALO - 1:SHIFT_HALO, :]
    first_row = lax.broadcasted_iota(jnp.int32, (tm, d), 0) == 0
    hh = jnp.where(first_row, prev_last, pltpu.roll(h, 1, axis=0)) - h
    carry_scr[...] = h[tm - SHIFT_HALO:tm, :]
    mix = lambda i: (h + hh * mu_ref[i:i + 1, :]).astype(BF16)
    xr, xw, xk, xv, xa, xg = [mix(i) for i in range(6)]
    lora_w = _dot(xw, w1_ref[...])
    lora_a = _dot(xa, a1_ref[...])
    lora_g = _dot(xg, g1_ref[...])
    r = _dot(xr, wr_ref[...])
    k = _dot(xk, wk_ref[...])
    v = _dot(xv, wv_ref[...])
    z = w0_ref[...] + _dot(jnp.tanh(lora_w).astype(BF16), w2_ref[...])
    lw = -math.exp(-0.5) * jax.nn.sigmoid(z)
    a = jax.nn.sigmoid(a0_ref[...] + _dot(lora_a.astype(BF16), a2_ref[...]))
    g = _dot(jax.nn.sigmoid(lora_g).astype(BF16), g2_ref[...])
    kkr = k * kk_ref[...]
    kk = kkr * lax.rsqrt(jnp.maximum(_headsum(kkr * kkr, ones_ref), 1e-24))
    k2 = k * (1.0 + (a - 1.0) * ka_ref[...])
    bonus_out[0] = _headsum(r * k2 * rk_ref[...], ones_ref) * v
    g_out[0] = g
    v_out[0] = v.astype(BF16)

    lw_hi, lw_lo = _hilo(lw)
    cum = _dot(tri_ref[...], lw_hi) + _dot(tri_ref[...], lw_lo)
    e_pos = jnp.exp(cum)
    e_neg = jnp.exp(-cum)
    at = (-kk * jnp.exp(cum - lw)).astype(BF16)
    rt = (r * e_pos).astype(BF16)
    bt = (kk * a * e_neg).astype(BF16)
    kt = (k2 * e_neg).astype(BF16)
    for c in range(tm // L):
        tok = slice(c * L, (c + 1) * L)
        ar_out[0, 2 * c * L:(2 * c + 1) * L, :] = at[tok]
        ar_out[0, (2 * c + 1) * L:(2 * c + 2) * L, :] = rt[tok]
        bk_out[0, 2 * c * L:(2 * c + 1) * L, :] = bt[tok]
        bk_out[0, (2 * c + 1) * L:(2 * c + 2) * L, :] = kt[tok]
        pl_out[0, c * SUBLANES:(c + 1) * SUBLANES, :] = jnp.broadcast_to(e_pos[(c + 1) * L - 1:(c + 1) * L, :], (SUBLANES, d))


def _pad_cols(w, n):
    return jnp.pad(w, ((0, 0), (0, n - w.shape[1])))


def _pad_rows(w, n):
    return jnp.pad(w, ((0, n - w.shape[0]), (0, 0)))


def _head_ones(d):
    w = min(MXU_TILE, d)
    assert d % w == 0 and w % RWKV_HEAD == 0
    idx = jnp.arange(w) // RWKV_HEAD
    return (idx[:, None] == idx[None, :]).astype(BF16)


def _rwkv_in(x, norm_g, time_mu, w_r, w_k, w_v, w0, w1, w2, a0, a1, a2, g1, g2, k_k, k_a, r_k, *, tm=512):
    b, s, d = x.shape
    tm = min(tm, s)
    L = SCAN_CHUNK
    assert tm % L == 0 and s % tm == 0
    row = lambda v: v.reshape(1, -1)
    lp = lambda n: ((n + LANES - 1) // LANES) * LANES
    idx = jnp.arange(tm)
    tri = ((idx[:, None] // L == idx[None, :] // L) & (idx[None, :] <= idx[:, None])).astype(BF16)
    consts = [row(norm_g), time_mu, w_r.astype(BF16), w_k.astype(BF16), w_v.astype(BF16),
              row(w0), _pad_cols(w1, lp(w1.shape[1])).astype(BF16), _pad_rows(w2, lp(w2.shape[0])).astype(BF16),
              row(a0), _pad_cols(a1, lp(a1.shape[1])).astype(BF16), _pad_rows(a2, lp(a2.shape[0])).astype(BF16),
              _pad_cols(g1, lp(g1.shape[1])).astype(BF16), _pad_rows(g2, lp(g2.shape[0])).astype(BF16),
              row(k_k), row(k_a), row(r_k), _head_ones(d), tri]
    tok = lambda rows: pl.BlockSpec((1, rows, d), lambda bi, si: (bi, si, 0))
    pl_rows = tm // L * SUBLANES
    return pl.pallas_call(
        functools.partial(_rwkv_in_body, tm=tm),
        grid=(b, s // tm),
        in_specs=[tok(tm)] + [_const_spec(c.shape) for c in consts],
        out_specs=[tok(2 * tm), tok(2 * tm), tok(tm), tok(pl_rows), tok(tm), tok(tm)],
        out_shape=[jax.ShapeDtypeStruct((b, 2 * s, d), BF16), jax.ShapeDtypeStruct((b, 2 * s, d), BF16),
                   jax.ShapeDtypeStruct((b, s, d), BF16), jax.ShapeDtypeStruct((b, s // L * SUBLANES, d), F32),
                   jax.ShapeDtypeStruct((b, s, d), F32), jax.ShapeDtypeStruct((b, s, d), F32)],
        scratch_shapes=[pltpu.VMEM((SHIFT_HALO, d), F32)],
        compiler_params=_params(("parallel", "arbitrary")),
        name="rwkv_in",
    )(x, *consts)


def _rwkv_scan_body(ar_ref, bk_ref, v_ref, pl_ref, y_ref, ht_scr, *, tile, heads):
    L = SCAN_CHUNK
    gk = heads * RWKV_HEAD
    gl = heads * L
    nb, groups = ht_scr.shape[0], ht_scr.shape[1]
    chains = [(bi, g) for bi in range(nb) for g in range(groups)]
    row = lax.broadcasted_iota(jnp.int32, (L, gl), 0)
    pos = lax.broadcasted_iota(jnp.int32, (L, gl), 1) % L
    strict = pos < row
    incl = pos <= row
    blk = lambda shape: (lax.broadcasted_iota(jnp.int32, shape, 0) // L) % heads == lax.broadcasted_iota(jnp.int32, shape, 1) // L
    bd_mask = blk((gl, gk))
    bd2_mask = blk((2 * gl, gk))

    @pl.when(pl.program_id(1) == 0)
    def _():
        ht_scr[...] = jnp.zeros_like(ht_scr)

    def bd(x):
        return jnp.where(bd_mask, jnp.concatenate([x] * heads, axis=0), 0).astype(BF16)

    def chunk(c, carry):
        each = lambda f, *xs: [f(*a) for a in zip(*xs)]
        cat = lambda *xs: jnp.concatenate(xs, axis=0)
        load = lambda ref, n: [ref[bi, pl.ds(pl.multiple_of(c * n, n), n), g * gk:(g + 1) * gk] for bi, g in chains]
        lhs = load(ar_ref, 2 * L)
        bk = load(bk_ref, 2 * L)
        v = load(v_ref, L)
        p_last = [x[0:1, :] for x in load(pl_ref, SUBLANES)]
        ht = [ht_scr[bi, g] for bi, g in chains]
        abk = each(lambda l_, x: _dot_nt(l_, jnp.where(bd2_mask, cat(*([x[:L]] * heads + [x[L:]] * heads)), 0)), lhs, bk)
        xr = each(lambda l_, h_: _dot_nt(l_, h_.astype(BF16)), lhs, ht)
        a_ab = [jnp.where(strict, x[:L, :gl], 0.0) for x in abk]
        a_rb = [jnp.where(incl, x[L:, :gl], 0.0) for x in abk]
        av = each(lambda x, v_: _dot(cat(jnp.where(strict, x[:L, gl:], 0.0), jnp.where(incl, x[L:, gl:], 0.0)).astype(BF16), bd(v_)),
                  abk, v)
        x2 = each(lambda x, a_: x[:L] + a_[:L], xr, av)
        step = lambda p_, x_: _dot(p_.astype(BF16), jnp.concatenate([bd(p_), bd(x_)], axis=1))
        res = each(step, a_ab, x2)
        rounds = int(math.log2(L)) - 1
        for i in range(rounds):
            p = [x[:, :gl] for x in res]
            x2 = each(lambda x_, r_: x_ + r_[:, gl:], x2, res)
            if i < rounds - 1:
                res = each(step, p, x2)
        u = each(lambda x_, p_: x_ + _dot(p_.astype(BF16), bd(x_)), x2, p)
        yu = each(lambda a_, u_: _dot(a_.astype(BF16), bd(u_)), a_rb, u)
        upd = each(lambda u_, v_, x: _dot_tn(cat(u_.astype(BF16), v_), x), u, v, bk)
        for i, (bi, g) in enumerate(chains):
            y_ref[bi, pl.ds(pl.multiple_of(c * L, L), L), g * gk:(g + 1) * gk] = xr[i][L:] + av[i][L:] + yu[i]
            ht_scr[bi, g] = (ht[i] + jnp.where(bd_mask, upd[i], 0.0)) * p_last[i]
        return carry

    lax.fori_loop(0, tile // L, chunk, 0)


def _rwkv_scan(ar, bk, v, p_last, *, nb=4, tile=256):
    b, s, d = v.shape
    gk = SCAN_HEADS * RWKV_HEAD
    L = SCAN_CHUNK
    tile = min(tile, s)
    nb = min(nb, b)
    assert L == RWKV_HEAD and tile % L == 0 and s % tile == 0 and d % gk == 0 and b % nb == 0
    spec = lambda rows: pl.BlockSpec((nb, rows, d), lambda bi, si: (bi, si, 0))
    return pl.pallas_call(
        functools.partial(_rwkv_scan_body, tile=tile, heads=SCAN_HEADS),
        grid=(b // nb, s // tile),
        in_specs=[spec(2 * tile), spec(2 * tile), spec(tile), spec(tile // L * SUBLANES)],
        out_specs=spec(tile),
        out_shape=jax.ShapeDtypeStruct((b, s, d), F32),
        scratch_shapes=[pltpu.VMEM((nb, d // gk, gk, gk), F32)],
        compiler_params=_params(("parallel", "arbitrary")),
        name="rwkv_scan",
    )(ar, bk, v, p_last)


def _rwkv_mixer_out(y_ref, bonus_ref, g_ref, lng_ref, lnb_ref, wo_ref, ones_ref):
    y = y_ref[...]
    inv_n = 1.0 / RWKV_HEAD
    mu = _headsum(y, ones_ref) * inv_n
    dy = y - mu
    var = _headsum(dy * dy, ones_ref) * inv_n
    yn = dy * lax.rsqrt(var + RWKV_GN_EPS) * lng_ref[...] + lnb_ref[...]
    return _dot(((yn + bonus_ref[...]) * g_ref[...]).astype(BF16), wo_ref[...])


def kernel(x, positions, ffn_norm, ffn_w_gate, ffn_w_up, ffn_w_down, mix_norm_even, w_in, conv_w, conv_b, conv_ln_g, conv_ln_b, q_norm, w_uq, kv_norm, w_ukv, w_out, mix_norm_odd, time_mu, w_r, w_k, w_v, w_o, w0, w1, w2, a0, a1, a2, g1, g2, k_k, k_a, r_k, ln_x_g, ln_x_b, final_norm):
    b, s, d = x.shape
    depth = ffn_norm.shape[0]
    x2 = x.reshape(b * s, d)
    wg, wu, wd = ffn_w_gate.astype(BF16), ffn_w_up.astype(BF16), ffn_w_down.astype(BF16)
    for layer in range(depth):
        x2 = _ffn(x2, ffn_norm[layer, 0], wg, wu, wd, (layer, 0))
        if layer % 2 == 0:
            e = layer // 2
            conv, q, k, v = _mix_in(x2.reshape(b, s, d), positions, mix_norm_even[e], w_in[e], conv_w[e], conv_b[e],
                                    conv_ln_g[e], conv_ln_b[e], q_norm[e], w_uq[e], kv_norm[e], w_ukv[e])
            attn = _attention(q, k, v)
            x2 = _mix_out(x2, conv.reshape(b * s, -1), attn.reshape(b * s, -1), w_out[e])
            mixer_out = None
        else:
            o = layer // 2
            ar, bk, v, p_last, bonus, g = _rwkv_in(
                x2.reshape(b, s, d), mix_norm_odd[o], time_mu[o], w_r[o], w_k[o], w_v[o], w0[o], w1[o], w2[o],
                a0[o], a1[o], a2[o], g1[o], g2[o], k_k[o], k_a[o], r_k[o].reshape(-1))
            y = _rwkv_scan(ar, bk, v, p_last)
            mixer_out = (y.reshape(b * s, d), bonus.reshape(b * s, d), g.reshape(b * s, d), ln_x_g[o], ln_x_b[o], w_o[o])
        last = layer == depth - 1
        x2 = _ffn(x2, ffn_norm[layer, 1], wg, wu, wd, (layer, 1), final_norm if last else None, mixer_out)
    return x2.reshape(b, s, d)
```
